```python
import math
import jax, jax.numpy as jnp
from jax import lax
import numpy as np

D_MODEL = 2048
BATCH = 4
SEQ = 2048
DEPTH = 4
DEC_BATCH = 8
DEC_SEQ = 8
PAST_LEN = 16384
PAGE_SIZE = 128

N_MIXERS = 2
N_ATTN_LAYERS = (DEPTH + 1) // 2
N_SGU_LAYERS = DEPTH // 2
N_HEADS = 8
HEAD_DIM = D_MODEL // (2 * N_HEADS)
V_DIM = 2 * HEAD_DIM
QK_WIDTH = 2 * N_HEADS * HEAD_DIM
V_WIDTH = N_HEADS * V_DIM
ROPE_THETA = 10000.0
Q_BLOCK = 128
SGU_FFN = 6 * D_MODEL
SGU_HALF = SGU_FFN // 2
SGU_GROUPS = 8
SGU_GROUP_DIM = SGU_HALF // SGU_GROUPS
CHUNK = 128
D_FF = 5632
N_SUBLAYERS = 3
N_MOD = 3 * N_SUBLAYERS
EPS = 1e-6

kernel_name = "hybrid_diffattn_chunk_sgu_macaron_adaln_step"


def rms_norm(x, g):
    xf = x.astype(jnp.float32)
    y = xf * lax.rsqrt(jnp.mean(xf * xf, axis=-1, keepdims=True) + EPS)
    return (y * g.astype(jnp.float32)).astype(x.dtype)


def layer_norm(x, g, b):
    xf = x.astype(jnp.float32)
    xc = xf - jnp.mean(xf, axis=-1, keepdims=True)
    y = xc * lax.rsqrt(jnp.mean(xc * xc, axis=-1, keepdims=True) + EPS)
    return (y * g.astype(jnp.float32) + b.astype(jnp.float32)).astype(x.dtype)


def adaln_params(c, w_ada, b_ada):
    mod = jax.nn.silu(c) @ w_ada + b_ada
    return jnp.split(mod[:, None, :], N_MOD, axis=-1)


def modulate(x, g, shift, scale):
    return rms_norm(x, g) * (1 + scale) + shift


def swiglu_ffn(h, w_in, w_out):
    gate, up = jnp.split(h @ w_in, 2, axis=-1)
    return (jax.nn.silu(gate) * up) @ w_out


def ffn_half_step(x, shift, scale, gate, g, w_in, w_out):
    return x + 0.5 * gate * swiglu_ffn(modulate(x, g, shift, scale), w_in, w_out)


def rotary(x, pos):
    half = HEAD_DIM // 2
    inv_freq = ROPE_THETA ** (-jnp.arange(half, dtype=jnp.float32) / half)
    ang = pos.astype(jnp.float32)[:, None] * inv_freq[None, :]
    cos = jnp.cos(ang)[None, :, None, :]
    sin = jnp.sin(ang)[None, :, None, :]
    xf = x.astype(jnp.float32)
    x1, x2 = xf[..., :half], xf[..., half:]
    return jnp.concatenate([x1 * cos - x2 * sin, x2 * cos + x1 * sin], axis=-1).astype(x.dtype)


def diff_qkv(h, pos, w_qkv):
    B, T, _ = h.shape
    q, k, v = jnp.split(h @ w_qkv, [QK_WIDTH, 2 * QK_WIDTH], axis=-1)
    q = rotary(q.reshape(B, T, 2 * N_HEADS, HEAD_DIM), pos)
    k = rotary(k.reshape(B, T, 2 * N_HEADS, HEAD_DIM), pos)
    v = v.reshape(B, T, N_HEADS, V_DIM)
    return q, k, v


def diff_lambda(lam_vecs, lam_init):
    lv = lam_vecs.astype(jnp.float32)
    return jnp.exp(jnp.sum(lv[0] * lv[1])) - jnp.exp(jnp.sum(lv[2] * lv[3])) + lam_init


def diff_attend(q, k, v, q_pos, k_pos, lam):
    B, Tq = q.shape[:2]
    Tk = k.shape[1]
    qh = q.reshape(B, Tq, N_HEADS, 2, HEAD_DIM)
    kh = k.reshape(B, Tk, N_HEADS, 2, HEAD_DIM)
    s = jnp.einsum("bqhcd,bkhcd->bhcqk", qh, kh, preferred_element_type=jnp.float32) * (HEAD_DIM ** -0.5)
    causal = k_pos[None, :] <= q_pos[:, None]
    s = jnp.where(causal, s, -jnp.inf)
    p = jax.nn.softmax(s, axis=-1)
    a = p[:, :, 0] - lam * p[:, :, 1]
    return jnp.einsum("bhqk,bkhe->bqhe", a.astype(v.dtype), v)


def diff_output(o, subln_g, lam_init, w_o):
    B, T = o.shape[:2]
    o = rms_norm(o, subln_g) * (1.0 - lam_init)
    return o.reshape(B, T, V_WIDTH) @ w_o


def sgu_mixer(h, w_in, ln_g, ln_b, w_s, b_s, w_out):
    B, T, _ = h.shape
    u, v = jnp.split(jax.nn.gelu(h @ w_in, approximate=False), 2, axis=-1)
    v = layer_norm(v, ln_g, ln_b)
    n_chunks = -(-T // CHUNK)
    pad = n_chunks * CHUNK - T
    vc = jnp.pad(v, ((0, 0), (0, pad), (0, 0))).reshape(B, n_chunks, CHUNK, SGU_GROUPS, SGU_GROUP_DIM)
    tri = jnp.tril(jnp.ones((CHUNK, CHUNK), dtype=bool))
    ws = jnp.where(tri[None], w_s, 0).astype(v.dtype)
    s = jnp.einsum("gts,bnsgc->bntgc", ws, vc) + b_s.T[None, None, :, :, None]
    s = s.reshape(B, n_chunks * CHUNK, SGU_HALF)[:, :T]
    return (u * s) @ w_out, v


def setup_inputs(seed: int = 0) -> dict:
    key = jax.random.key(seed)
    ks = jax.random.split(key, 24)
    f32 = jnp.float32
    n_pages = PAST_LEN // PAGE_SIZE
    n_used = DEC_BATCH * n_pages
    n_phys = n_used + max(1, n_used // 4)

    def nrm(k, shape, scale=1.0):
        return jax.random.normal(k, shape, f32) * scale

    page_table = jax.random.permutation(ks[6], n_phys)[:n_used].reshape(DEC_BATCH, n_pages).astype(jnp.int32)
    return {
        "x_prompt": nrm(ks[0], (BATCH, SEQ, D_MODEL)),
        "x_sample": nrm(ks[1], (DEC_BATCH, DEC_SEQ, D_MODEL)),
        "c_prompt": nrm(ks[2], (BATCH, D_MODEL)),
        "c_sample": nrm(ks[3], (DEC_BATCH, D_MODEL)),
        "cache_k": nrm(ks[4], (N_ATTN_LAYERS, n_phys, PAGE_SIZE, 2 * N_HEADS, HEAD_DIM)),
        "cache_v": nrm(ks[5], (N_ATTN_LAYERS, n_phys, PAGE_SIZE, N_HEADS, V_DIM)),
        "page_table": page_table,
        "norm_g": 1.0 + nrm(ks[7], (DEPTH, N_SUBLAYERS, D_MODEL), 0.01),
        "w_ada": nrm(ks[8], (DEPTH, D_MODEL, N_MOD * D_MODEL), 0.5 * D_MODEL ** -0.5),
        "b_ada": nrm(ks[9], (DEPTH, N_MOD * D_MODEL), 0.01),
        "ffn_w_in": nrm(ks[10], (DEPTH, 2, D_MODEL, 2 * D_FF), D_MODEL ** -0.5),
        "ffn_w_out": nrm(ks[11], (DEPTH, 2, D_FF, D_MODEL), D_FF ** -0.5),
        "attn_w_qkv": nrm(ks[12], (N_ATTN_LAYERS, D_MODEL, 2 * QK_WIDTH + V_WIDTH), D_MODEL ** -0.5),
        "attn_w_o": nrm(ks[13], (N_ATTN_LAYERS, V_WIDTH, D_MODEL), V_WIDTH ** -0.5),
        "attn_lambda": nrm(ks[14], (N_ATTN_LAYERS, 4, HEAD_DIM), 0.1),
        "attn_subln_g": 1.0 + nrm(ks[15], (N_ATTN_LAYERS, V_DIM), 0.01),
        "sgu_w_in": nrm(ks[16], (N_SGU_LAYERS, D_MODEL, SGU_FFN), D_MODEL ** -0.5),
        "sgu_ln_g": 1.0 + nrm(ks[17], (N_SGU_LAYERS, SGU_HALF), 0.01),
        "sgu_ln_b": nrm(ks[18], (N_SGU_LAYERS, SGU_HALF), 0.01),
        "sgu_w_s": nrm(ks[19], (N_SGU_LAYERS, SGU_GROUPS, CHUNK, CHUNK), CHUNK ** -0.5),
        "sgu_b_s": 1.0 + nrm(ks[20], (N_SGU_LAYERS, SGU_GROUPS, CHUNK), 0.01),
        "sgu_w_out": nrm(ks[21], (N_SGU_LAYERS, SGU_HALF, D_MODEL), SGU_HALF ** -0.5),
        "final_g": 1.0 + nrm(ks[22], (D_MODEL,), 0.01),
    }


def reference(x_prompt, x_sample, c_prompt, c_sample, cache_k, cache_v, page_table, norm_g, w_ada, b_ada,
              ffn_w_in, ffn_w_out, attn_w_qkv, attn_w_o, attn_lambda, attn_subln_g, sgu_w_in, sgu_ln_g,
              sgu_ln_b, sgu_w_s, sgu_b_s, sgu_w_out, final_g):
    B, T, _ = x_prompt.shape
    DB, TS, _ = x_sample.shape
    n_pages = page_table.shape[1]
    past = n_pages * PAGE_SIZE
    pos_p = jnp.arange(T, dtype=jnp.int32)
    pos_s = past + jnp.arange(TS, dtype=jnp.int32)
    pos_all = jnp.arange(past + TS, dtype=jnp.int32)
    n_qblocks = T // Q_BLOCK

    xp, xs = x_prompt, x_sample
    new_kp, new_vp, new_ks, new_vs, new_sgu_v = [], [], [], [], []
    for i in range(DEPTH):
        mp = adaln_params(c_prompt, w_ada[i], b_ada[i])
        ms = adaln_params(c_sample, w_ada[i], b_ada[i])
        xp = ffn_half_step(xp, mp[0], mp[1], mp[2], norm_g[i, 0], ffn_w_in[i, 0], ffn_w_out[i, 0])
        xs = ffn_half_step(xs, ms[0], ms[1], ms[2], norm_g[i, 0], ffn_w_in[i, 0], ffn_w_out[i, 0])
        hp = modulate(xp, norm_g[i, 1], mp[3], mp[4])
        hs = modulate(xs, norm_g[i, 1], ms[3], ms[4])
        if i % N_MIXERS == 0:
            a = i // N_MIXERS
            lam_init = 0.8 - 0.6 * math.exp(-0.3 * i)
            lam = diff_lambda(attn_lambda[a], lam_init)
            qp, kp, vp = diff_qkv(hp, pos_p, attn_w_qkv[a])
            qb = jnp.moveaxis(qp.reshape(B, n_qblocks, Q_BLOCK, 2 * N_HEADS, HEAD_DIM), 1, 0)
            op = lax.map(lambda args: diff_attend(args[0], kp, vp, args[1], pos_p, lam),
                         (qb, pos_p.reshape(n_qblocks, Q_BLOCK)))
            op = jnp.moveaxis(op, 0, 1).reshape(B, T, N_HEADS, V_DIM)
            mix_p = diff_output(op, attn_subln_g[a], lam_init, attn_w_o[a])
            qs, ks_, vs_ = diff_qkv(hs, pos_s, attn_w_qkv[a])
            k_past = cache_k[a, page_table].reshape(DB, past, 2 * N_HEADS, HEAD_DIM)
            v_past = cache_v[a, page_table].reshape(DB, past, N_HEADS, V_DIM)
            k_all = jnp.concatenate([k_past, ks_], axis=1)
            v_all = jnp.concatenate([v_past, vs_], axis=1)
            os_ = diff_attend(qs, k_all, v_all, pos_s, pos_all, lam)
            mix_s = diff_output(os_, attn_subln_g[a], lam_init, attn_w_o[a])
            new_kp.append(kp)
            new_vp.append(vp)
            new_ks.append(ks_)
            new_vs.append(vs_)
        else:
            g = i // N_MIXERS
            mix_p, _ = sgu_mixer(hp, sgu_w_in[g], sgu_ln_g[g], sgu_ln_b[g], sgu_w_s[g], sgu_b_s[g], sgu_w_out[g])
            mix_s, v_rows = sgu_mixer(hs, sgu_w_in[g], sgu_ln_g[g], sgu_ln_b[g], sgu_w_s[g], sgu_b_s[g], sgu_w_out[g])
            new_sgu_v.append(v_rows)
        xp = xp + mp[5] * mix_p
        xs = xs + ms[5] * mix_s
        xp = ffn_half_step(xp, mp[6], mp[7], mp[8], norm_g[i, 2], ffn_w_in[i, 1], ffn_w_out[i, 1])
        xs = ffn_half_step(xs, ms[6], ms[7], ms[8], norm_g[i, 2], ffn_w_in[i, 1], ffn_w_out[i, 1])

    y_prompt = rms_norm(xp, final_g)
    y_sample = rms_norm(xs, final_g)
    k_prompt = jnp.stack(new_kp)
    v_prompt = jnp.stack(new_vp)
    k_sample = jnp.stack(new_ks)
    v_sample = jnp.stack(new_vs)
    sgu_v_sample = jnp.stack(new_sgu_v)
    return (y_prompt, y_sample, k_prompt, v_prompt, k_sample, v_sample, sgu_v_sample)
```

```python
import functools
import math

import jax
import jax.numpy as jnp
from jax import lax
from jax.experimental import pallas as pl
from jax.experimental.pallas import tpu as pltpu

F32 = jnp.float32
BF16 = jnp.bfloat16

EPS = 1e-6
ROPE_THETA = 10000.0
HEAD_DIM = 128
V_DIM = 2 * HEAD_DIM
CHUNK = 128
SGU_GROUPS = 8
PAGE_SIZE = 128
N_MOD = 9
N_SEQ_PAD = 16

VMEM_LIMIT_BYTES = 60 * 1024 * 1024
ROW_TILE = 1024
PAGES_PER_STEP = 4


def _cparams(sem):
    return pltpu.CompilerParams(dimension_semantics=sem, vmem_limit_bytes=VMEM_LIMIT_BYTES)


def _dot(a, b):
    return jnp.dot(a, b, preferred_element_type=F32)


def _dot_nt(a, b):
    return lax.dot_general(a, b, (((1,), (1,)), ((), ())), preferred_element_type=F32)


def _rms(x, g):
    return x * lax.rsqrt(jnp.mean(x * x, axis=-1, keepdims=True) + EPS) * g


def _mod_row(mod_ref, k, seq, per_row):
    if per_row:
        return mod_ref[k]
    return mod_ref[k, pl.ds(seq, 1), :]


def _ada_kernel(c_ref, w_ref, b_ref, o_ref):
    c = c_ref[...]
    a = (c * jax.nn.sigmoid(c)).astype(BF16)
    o_ref[...] = _dot(a, w_ref[...].astype(BF16)) + b_ref[...]


def _ada_call(c_all, w_ada, b_ada):
    depth, d, _ = w_ada.shape
    tn = 1024
    nt = d // tn
    b4 = b_ada.reshape(depth, N_MOD, 1, d)
    return pl.pallas_call(
        _ada_kernel,
        grid=(depth, N_MOD * nt),
        in_specs=[
            pl.BlockSpec((N_SEQ_PAD, d), lambda l, n: (0, 0)),
            pl.BlockSpec((None, d, tn), lambda l, n: (l, 0, n)),
            pl.BlockSpec((None, None, 1, tn), lambda l, n: (l, n // nt, 0, n % nt)),
        ],
        out_specs=pl.BlockSpec((None, None, N_SEQ_PAD, tn), lambda l, n: (l, n // nt, 0, n % nt)),
        out_shape=jax.ShapeDtypeStruct((depth, N_MOD, N_SEQ_PAD, d), F32),
        compiler_params=_cparams(("arbitrary", "arbitrary")),
        name="adaln",
    )(c_all, w_ada, b4)


def _ffn_kernel(x_ref, mod_ref, g_ref, wg_ref, wu_ref, wo_ref, fg_ref, o_ref, h_ref, *,
                per_row, tiles_per_seq, final_norm, row_chunk):
    i = pl.program_id(0)
    j = pl.program_id(1)
    nj = pl.num_programs(1)
    seq = i // tiles_per_seq

    @pl.when(j == 0)
    def _():
        shift = _mod_row(mod_ref, 0, seq, per_row)
        scale = _mod_row(mod_ref, 1, seq, per_row)
        h = _rms(x_ref[...], g_ref[...]) * (1.0 + scale) + shift
        h_ref[...] = h.astype(BF16)
        o_ref[...] = jnp.zeros(o_ref.shape, F32)

    wg = wg_ref[...].astype(BF16)
    wu = wu_ref[...].astype(BF16)
    wo = wo_ref[...].astype(BF16)
    for r in range(0, h_ref.shape[0], row_chunk):
        rows = slice(r, r + row_chunk)
        h = h_ref[rows, :]
        gate = _dot(h, wg)
        up = _dot(h, wu)
        a = (gate * jax.nn.sigmoid(gate) * up).astype(BF16)
        o_ref[rows, :] += _dot(a, wo)

    @pl.when(j == nj - 1)
    def _():
        gmod = _mod_row(mod_ref, 2, seq, per_row)
        y = x_ref[...] + 0.5 * gmod * o_ref[...]
        if final_norm:
            y = _rms(y, fg_ref[...])
        o_ref[...] = y


def _ffn_call(x, mod, norm_g, w_in, w_out, final_g, *, layer, sub, which, tm, per_row,
              tiles_per_seq, final_norm):
    m, d = x.shape
    d_ff = w_out.shape[2]
    tf = 256
    nj = d_ff // tf
    rows = mod.shape[2]
    g3 = norm_g.reshape(norm_g.shape[0], norm_g.shape[1], 1, d)
    kern = functools.partial(_ffn_kernel, per_row=per_row, tiles_per_seq=tiles_per_seq,
                             final_norm=final_norm, row_chunk=min(tm, 256))
    return pl.pallas_call(
        kern,
        grid=(m // tm, nj),
        in_specs=[
            pl.BlockSpec((tm, d), lambda i, j: (i, 0), pipeline_mode=pl.Buffered(1)),
            pl.BlockSpec((None, 3, rows, d), lambda i, j: (layer, sub, 0, 0)),
            pl.BlockSpec((None, None, 1, d), lambda i, j: (layer, sub, 0, 0)),
            pl.BlockSpec((None, None, d, tf), lambda i, j: (layer, which, 0, j)),
            pl.BlockSpec((None, None, d, tf), lambda i, j: (layer, which, 0, nj + j)),
            pl.BlockSpec((None, None, tf, d), lambda i, j: (layer, which, j, 0)),
            pl.BlockSpec((1, d), lambda i, j: (0, 0)),
        ],
        out_specs=pl.BlockSpec((tm, d), lambda i, j: (i, 0)),
        out_shape=jax.ShapeDtypeStruct((m, d), F32),
        scratch_shapes=[pltpu.VMEM((tm, d), BF16)],
        compiler_params=_cparams(("arbitrary", "arbitrary")),
        name="ffn_half_step",
    )(x, mod, g3, w_in, w_in, w_out, final_g.reshape(1, d))


def _qkv_kernel(x_ref, mod_ref, g_ref, w_ref, cs_ref, q_ref, k_ref, kb_ref, v_ref, vb_ref, h_ref, *,
                per_row, tiles_per_seq, tn):
    i = pl.program_id(0)
    p = pl.program_id(1)
    n = pl.program_id(2)
    seq = i // tiles_per_seq

    @pl.when((p == 0) & (n == 0))
    def _():
        shift = _mod_row(mod_ref, 0, seq, per_row)
        scale = _mod_row(mod_ref, 1, seq, per_row)
        h = _rms(x_ref[...], g_ref[...]) * (1.0 + scale) + shift
        h_ref[...] = h.astype(BF16)

    acc = _dot(h_ref[...], w_ref[...].astype(BF16))

    def rotary(dst_refs):
        cos2 = cs_ref[:, :HEAD_DIM]
        sin2 = cs_ref[:, HEAD_DIM:]
        for hh in range(tn // HEAD_DIM):
            sl = slice(hh * HEAD_DIM, (hh + 1) * HEAD_DIM)
            xh = acc[:, sl]
            r = xh * cos2 + pltpu.roll(xh, HEAD_DIM // 2, 1) * sin2
            for ref in dst_refs:
                ref[:, sl] = r.astype(ref.dtype)

    @pl.when(p == 0)
    def _():
        rotary([q_ref])

    @pl.when(p == 1)
    def _():
        rotary([k_ref, kb_ref])

    @pl.when(p == 2)
    def _():
        v_ref[...] = acc
        vb_ref[...] = acc.astype(BF16)


def _qkv_call(x, mod, norm_g, w_qkv, cs, *, layer, a, tm, per_row, tiles_per_seq):
    m, d = x.shape
    tn = 512
    nt = d // tn
    rows = mod.shape[2]
    g3 = norm_g.reshape(norm_g.shape[0], norm_g.shape[1], 1, d)
    n_cs_tiles = cs.shape[0] // tm
    kern = functools.partial(_qkv_kernel, per_row=per_row, tiles_per_seq=tiles_per_seq, tn=tn)

    def part_map(part):
        def index(i, p, n):
            return (i, jnp.where(p < part, 0, jnp.where(p == part, n, nt - 1)))
        return index

    out_block = lambda part: pl.BlockSpec((tm, tn), part_map(part))
    return pl.pallas_call(
        kern,
        grid=(m // tm, 3, nt),
        in_specs=[
            pl.BlockSpec((tm, d), lambda i, p, n: (i, 0)),
            pl.BlockSpec((None, 3, rows, d), lambda i, p, n: (layer, 1, 0, 0)),
            pl.BlockSpec((None, None, 1, d), lambda i, p, n: (layer, 1, 0, 0)),
            pl.BlockSpec((None, d, tn), lambda i, p, n: (a, 0, p * nt + n)),
            pl.BlockSpec((tm, 2 * HEAD_DIM), lambda i, p, n: (i % n_cs_tiles, 0)),
        ],
        out_specs=[out_block(0), out_block(1), out_block(1), out_block(2), out_block(2)],
        out_shape=[
            jax.ShapeDtypeStruct((m, d), BF16),
            jax.ShapeDtypeStruct((m, d), F32),
            jax.ShapeDtypeStruct((m, d), BF16),
            jax.ShapeDtypeStruct((m, d), F32),
            jax.ShapeDtypeStruct((m, d), BF16),
        ],
        scratch_shapes=[pltpu.VMEM((tm, d), BF16)],
        compiler_params=_cparams(("arbitrary", "arbitrary", "arbitrary")),
        name="attn_qkv",
    )(x, mod, g3, w_qkv, cs)


def _lambda_full(lam_ref, lam_init):
    lv = lam_ref[...]
    d1 = jnp.sum(lv[0:1] * lv[1:2], axis=-1, keepdims=True)
    d2 = jnp.sum(lv[2:3] * lv[3:4], axis=-1, keepdims=True)
    return jnp.exp(d1) - jnp.exp(d2) + lam_init


def _softmax_step(s, m_old, l_old):
    m_new = jnp.maximum(m_old, jnp.max(s, axis=-1, keepdims=True))
    p = jnp.exp(s - m_new)
    alpha = jnp.exp(m_old - m_new)
    l_new = alpha * l_old + jnp.sum(p, axis=-1, keepdims=True)
    return p, alpha, m_new, l_new


def _attn_kernel(q_ref, k_ref, v_ref, lam_ref, sg_ref, o_ref, *, tq, tk, lam_init):
    qi = pl.program_id(2)
    q = q_ref[...]
    q1 = q[:, :HEAD_DIM]
    q2 = q[:, HEAD_DIM:]
    scale = HEAD_DIM ** -0.5
    row = qi * tq + lax.broadcasted_iota(jnp.int32, (tq, tk), 0)
    n_kv = (qi * tq + tq) // tk

    def body(j, carry):
        m1, l1, a1, m2, l2, a2 = carry
        start = pl.multiple_of(j * tk, tk)
        kb = k_ref[pl.ds(start, tk), :]
        vb = v_ref[pl.ds(start, tk), :]
        col = j * tk + lax.broadcasted_iota(jnp.int32, (tq, tk), 1)
        visible = col <= row
        s1 = jnp.where(visible, _dot_nt(q1, kb[:, :HEAD_DIM]) * scale, -jnp.inf)
        s2 = jnp.where(visible, _dot_nt(q2, kb[:, HEAD_DIM:]) * scale, -jnp.inf)
        p1, al1, m1, l1 = _softmax_step(s1, m1, l1)
        p2, al2, m2, l2 = _softmax_step(s2, m2, l2)
        a1 = al1 * a1 + _dot(p1.astype(BF16), vb)
        a2 = al2 * a2 + _dot(p2.astype(BF16), vb)
        return m1, l1, a1, m2, l2, a2

    neg = jnp.full((tq, 1), -jnp.inf, F32)
    zero = jnp.zeros((tq, 1), F32)
    zacc = jnp.zeros((tq, V_DIM), F32)
    m1, l1, a1, m2, l2, a2 = lax.fori_loop(0, n_kv, body, (neg, zero, zacc, neg, zero, zacc))
    lam = _lambda_full(lam_ref, lam_init)
    o = a1 / l1 - lam * (a2 / l2)
    o_ref[...] = (_rms(o, sg_ref[...]) * (1.0 - lam_init)).astype(o_ref.dtype)


def _attn_call(q, kb, vb, lam_vecs, subln_g, *, n_seq, seq_len, lam_init):
    m, d = q.shape
    n_heads = d // V_DIM
    tq = tk = 256
    nq = seq_len // tq
    kern = functools.partial(_attn_kernel, tq=tq, tk=tk, lam_init=lam_init)
    return pl.pallas_call(
        kern,
        grid=(n_seq, n_heads, nq),
        in_specs=[
            pl.BlockSpec((tq, V_DIM), lambda b, h, qi: (b * nq + qi, h)),
            pl.BlockSpec((seq_len, V_DIM), lambda b, h, qi: (b, h)),
            pl.BlockSpec((seq_len, V_DIM), lambda b, h, qi: (b, h)),
            pl.BlockSpec((4, HEAD_DIM), lambda b, h, qi: (0, 0)),
            pl.BlockSpec((1, V_DIM), lambda b, h, qi: (0, 0)),
        ],
        out_specs=pl.BlockSpec((tq, V_DIM), lambda b, h, qi: (b * nq + qi, h)),
        out_shape=jax.ShapeDtypeStruct((m, d), BF16),
        compiler_params=_cparams(("arbitrary", "arbitrary", "arbitrary")),
        name="prompt_diff_attn",
    )(q, kb, vb, lam_vecs, subln_g.reshape(1, V_DIM))


def _decode_kernel(pt_ref, q_ref, kn_ref, vn_ref, lam_ref, sg_ref, *rest, n_heads, ts, lam_init):
    del pt_ref
    npg = PAGES_PER_STEP
    k_refs = rest[:npg]
    v_refs = rest[npg:2 * npg]
    o_ref, m_ref, l_ref, acc_ref = rest[2 * npg:]
    c = pl.program_id(1)
    nc = pl.num_programs(1)
    scale = HEAD_DIM ** -0.5

    @pl.when(c == 0)
    def _():
        m_ref[...] = jnp.full(m_ref.shape, -jnp.inf, F32)
        l_ref[...] = jnp.zeros(l_ref.shape, F32)
        acc_ref[...] = jnp.zeros(acc_ref.shape, F32)

    q = q_ref[...].astype(BF16)

    def update(hh, s, v):
        p, alpha, m_new, l_new = _softmax_step(s, m_ref[hh], l_ref[hh])
        m_ref[hh] = m_new
        l_ref[hh] = l_new
        acc_ref[hh] = alpha * acc_ref[hh] + _dot(p.astype(BF16), v)

    def scores(hh, k1, k2, mask=None):
        q1 = q[:, (2 * hh) * HEAD_DIM:(2 * hh + 1) * HEAD_DIM]
        q2 = q[:, (2 * hh + 1) * HEAD_DIM:(2 * hh + 2) * HEAD_DIM]
        s = jnp.concatenate([_dot_nt(q1, k1), _dot_nt(q2, k2)], axis=0) * scale
        if mask is not None:
            s = jnp.where(mask, s, -jnp.inf)
        return s

    for u in range(npg):
        for hh in range(n_heads):
            k1 = k_refs[u][:, 2 * hh, :].astype(BF16)
            k2 = k_refs[u][:, 2 * hh + 1, :].astype(BF16)
            v = v_refs[u][:, hh, :].astype(BF16)
            update(hh, scores(hh, k1, k2), v)

    @pl.when(c == nc - 1)
    def _():
        nk = kn_ref.shape[0]
        qrow = lax.broadcasted_iota(jnp.int32, (2 * ts, nk), 0) % ts
        kcol = lax.broadcasted_iota(jnp.int32, (2 * ts, nk), 1)
        mask = kcol <= qrow
        kn = kn_ref[...].astype(BF16)
        vn = vn_ref[...].astype(BF16)
        lam = _lambda_full(lam_ref, lam_init)
        for hh in range(n_heads):
            k1 = kn[:, (2 * hh) * HEAD_DIM:(2 * hh + 1) * HEAD_DIM]
            k2 = kn[:, (2 * hh + 1) * HEAD_DIM:(2 * hh + 2) * HEAD_DIM]
            update(hh, scores(hh, k1, k2, mask), vn[:, hh * V_DIM:(hh + 1) * V_DIM])
            on = acc_ref[hh] / l_ref[hh]
            o = on[:ts] - lam * on[ts:]
            o_ref[:, hh * V_DIM:(hh + 1) * V_DIM] = _rms(o, sg_ref[...]) * (1.0 - lam_init)


def _decode_call(page_table, q, k_new, v_new, lam_vecs, subln_g, cache_k, cache_v, *, a, lam_init):
    n_seq, n_pages = page_table.shape
    m, d = q.shape
    ts = m // n_seq
    n_heads = d // V_DIM
    npg = PAGES_PER_STEP
    nc = n_pages // npg
    pad = PAGE_SIZE - ts
    kn = jnp.pad(k_new.reshape(n_seq, ts, d), ((0, 0), (0, pad), (0, 0)))
    vn = jnp.pad(v_new.reshape(n_seq, ts, d), ((0, 0), (0, pad), (0, 0)))

    def page_spec(shape, u):
        return pl.BlockSpec((None, None) + shape,
                            lambda b, c, pt: (a, pt[b, c * npg + u], 0, 0, 0))

    k_specs = [page_spec(cache_k.shape[2:], u) for u in range(npg)]
    v_specs = [page_spec(cache_v.shape[2:], u) for u in range(npg)]
    kern = functools.partial(_decode_kernel, n_heads=n_heads, ts=ts, lam_init=lam_init)
    grid_spec = pltpu.PrefetchScalarGridSpec(
        num_scalar_prefetch=1,
        grid=(n_seq, nc),
        in_specs=[
            pl.BlockSpec((ts, d), lambda b, c, pt: (b, 0)),
            pl.BlockSpec((None, PAGE_SIZE, d), lambda b, c, pt: (b, 0, 0)),
            pl.BlockSpec((None, PAGE_SIZE, d), lambda b, c, pt: (b, 0, 0)),
            pl.BlockSpec((4, HEAD_DIM), lambda b, c, pt: (0, 0)),
            pl.BlockSpec((1, V_DIM), lambda b, c, pt: (0, 0)),
        ] + k_specs + v_specs,
        out_specs=pl.BlockSpec((ts, d), lambda b, c, pt: (b, 0)),
        scratch_shapes=[
            pltpu.VMEM((n_heads, 2 * ts, 1), F32),
            pltpu.VMEM((n_heads, 2 * ts, 1), F32),
            pltpu.VMEM((n_heads, 2 * ts, V_DIM), F32),
        ],
    )
    return pl.pallas_call(
        kern,
        grid_spec=grid_spec,
        out_shape=jax.ShapeDtypeStruct((m, d), F32),
        compiler_params=_cparams(("arbitrary", "arbitrary")),
        name="sample_paged_diff_attn",
    )(page_table, q, kn, vn, lam_vecs, subln_g.reshape(1, V_DIM),
      *([cache_k] * npg), *([cache_v] * npg))


def _proj_kernel(a_ref, w_ref, x_ref, mod_ref, o_ref, *, per_row, tiles_per_seq):
    seq = pl.program_id(0) // tiles_per_seq
    mix = _dot(a_ref[...].astype(BF16), w_ref[...].astype(BF16))
    o_ref[...] = x_ref[...] + _mod_row(mod_ref, 2, seq, per_row) * mix


def _proj_call(act, w, x, mod, *, layer, w_index, tm, tn, per_row, tiles_per_seq):
    m, d = x.shape
    kdim = act.shape[1]
    rows = mod.shape[2]
    kern = functools.partial(_proj_kernel, per_row=per_row, tiles_per_seq=tiles_per_seq)
    return pl.pallas_call(
        kern,
        grid=(m // tm, d // tn),
        in_specs=[
            pl.BlockSpec((tm, kdim), lambda i, n: (i, 0)),
            pl.BlockSpec((None, kdim, tn), lambda i, n: (w_index, 0, n)),
            pl.BlockSpec((tm, tn), lambda i, n: (i, n)),
            pl.BlockSpec((None, 3, rows, tn), lambda i, n: (layer, 1, 0, n)),
        ],
        out_specs=pl.BlockSpec((tm, tn), lambda i, n: (i, n)),
        out_shape=jax.ShapeDtypeStruct((m, d), F32),
        compiler_params=_cparams(("arbitrary", "arbitrary")),
        name="gated_residual_proj",
    )(act, w, x, mod)


def _sgu_in_kernel(x_ref, mod_ref, g_ref, w_ref, u_ref, v_ref, mean_ref, rstd_ref,
                   h_ref, s1_ref, s2_ref, *, per_row, tiles_per_seq, half):
    i = pl.program_id(0)
    n = pl.program_id(1)
    nn = pl.num_programs(1)
    nu = nn // 2
    seq = i // tiles_per_seq

    @pl.when(n == 0)
    def _():
        shift = _mod_row(mod_ref, 0, seq, per_row)
        scale = _mod_row(mod_ref, 1, seq, per_row)
        h = _rms(x_ref[...], g_ref[...]) * (1.0 + scale) + shift
        h_ref[...] = h.astype(BF16)
        s1_ref[...] = jnp.zeros(s1_ref.shape, F32)
        s2_ref[...] = jnp.zeros(s2_ref.shape, F32)

    z = _dot(h_ref[...], w_ref[...].astype(BF16))
    z = 0.5 * z * (1.0 + lax.erf(z * (2.0 ** -0.5)))

    @pl.when(n < nu)
    def _():
        u_ref[...] = z.astype(u_ref.dtype)

    @pl.when(n >= nu)
    def _():
        v_ref[...] = z
        s1_ref[...] += jnp.sum(z, axis=-1, keepdims=True)
        s2_ref[...] += jnp.sum(z * z, axis=-1, keepdims=True)

    @pl.when(n == nn - 1)
    def _():
        mean = s1_ref[...] * (1.0 / half)
        var = s2_ref[...] * (1.0 / half) - mean * mean
        mean_ref[...] = mean
        rstd_ref[...] = lax.rsqrt(var + EPS)


def _sgu_in_call(x, mod, norm_g, w_in, *, layer, g, tm, per_row, tiles_per_seq):
    m, d = x.shape
    ffn = w_in.shape[2]
    half = ffn // 2
    tn = 512
    nn = ffn // tn
    nu = nn // 2
    rows = mod.shape[2]
    g3 = norm_g.reshape(norm_g.shape[0], norm_g.shape[1], 1, d)
    kern = functools.partial(_sgu_in_kernel, per_row=per_row, tiles_per_seq=tiles_per_seq, half=half)
    return pl.pallas_call(
        kern,
        grid=(m // tm, nn),
        in_specs=[
            pl.BlockSpec((tm, d), lambda i, n: (i, 0)),
            pl.BlockSpec((None, 3, rows, d), lambda i, n: (layer, 1, 0, 0)),
            pl.BlockSpec((None, None, 1, d), lambda i, n: (layer, 1, 0, 0)),
            pl.BlockSpec((None, d, tn), lambda i, n: (g, 0, n)),
        ],
        out_specs=[
            pl.BlockSpec((tm, tn), lambda i, n: (i, jnp.minimum(n, nu - 1))),
            pl.BlockSpec((tm, tn), lambda i, n: (i, jnp.maximum(n - nu, 0))),
            pl.BlockSpec((tm, 1), lambda i, n: (i, 0)),
            pl.BlockSpec((tm, 1), lambda i, n: (i, 0)),
        ],
        out_shape=[
            jax.ShapeDtypeStruct((m, half), BF16),
            jax.ShapeDtypeStruct((m, half), F32),
            jax.ShapeDtypeStruct((m, 1), F32),
            jax.ShapeDtypeStruct((m, 1), F32),
        ],
        scratch_shapes=[pltpu.VMEM((tm, d), BF16), pltpu.VMEM((tm, 1), F32), pltpu.VMEM((tm, 1), F32)],
        compiler_params=_cparams(("arbitrary", "arbitrary")),
        name="sgu_in",
    )(x, mod, g3, w_in)


def _sgu_gate_kernel(u_ref, v_ref, mean_ref, rstd_ref, lg_ref, lb_ref, ws_ref, bs_ref,
                     o_ref, vn_ref, *, chunk_rows, n_chunks):
    vn = (v_ref[...] - mean_ref[...]) * rstd_ref[...] * lg_ref[...] + lb_ref[...]
    if vn_ref is not None:
        vn_ref[...] = vn
    vnb = vn.astype(BF16)
    cr = chunk_rows
    r = lax.broadcasted_iota(jnp.int32, (cr, cr), 0)
    c = lax.broadcasted_iota(jnp.int32, (cr, cr), 1)
    ws = jnp.where(c <= r, ws_ref[:cr, :cr], 0.0).astype(BF16)
    bs = bs_ref[:cr, :]
    for ch in range(n_chunks):
        sl = slice(ch * cr, (ch + 1) * cr)
        s = _dot(ws, vnb[sl]) + bs
        o_ref[sl, :] = (u_ref[sl, :].astype(F32) * s).astype(o_ref.dtype)


def _sgu_gate_call(u, v, mean, rstd, ln_g, ln_b, w_s, b_s, *, g, tm, chunk_rows, want_vn):
    m, half = u.shape
    gd = half // SGU_GROUPS
    n_chunks = tm // chunk_rows
    bs3 = b_s.reshape(b_s.shape[0], SGU_GROUPS, CHUNK, 1)
    lg3 = ln_g.reshape(ln_g.shape[0], 1, half)
    lb3 = ln_b.reshape(ln_b.shape[0], 1, half)

    def kern(*refs):
        if want_vn:
            _sgu_gate_kernel(*refs, chunk_rows=chunk_rows, n_chunks=n_chunks)
        else:
            _sgu_gate_kernel(*refs, None, chunk_rows=chunk_rows, n_chunks=n_chunks)

    blk = pl.BlockSpec((tm, gd), lambda i, j: (i, j))
    out_specs = [blk]
    out_shape = [jax.ShapeDtypeStruct((m, half), BF16)]
    if want_vn:
        out_specs.append(blk)
        out_shape.append(jax.ShapeDtypeStruct((m, half), F32))
    return pl.pallas_call(
        kern,
        grid=(m // tm, SGU_GROUPS),
        in_specs=[
            blk,
            blk,
            pl.BlockSpec((tm, 1), lambda i, j: (i, 0)),
            pl.BlockSpec((tm, 1), lambda i, j: (i, 0)),
            pl.BlockSpec((None, 1, gd), lambda i, j: (g, 0, j)),
            pl.BlockSpec((None, 1, gd), lambda i, j: (g, 0, j)),
            pl.BlockSpec((None, None, CHUNK, CHUNK), lambda i, j: (g, j, 0, 0)),
            pl.BlockSpec((None, None, CHUNK, 1), lambda i, j: (g, j, 0, 0)),
        ],
        out_specs=out_specs,
        out_shape=out_shape,
        compiler_params=_cparams(("arbitrary", "arbitrary")),
        name="sgu_gate",
    )(u, v, mean, rstd, lg3, lb3, w_s, bs3)


def _rotary_table(pos):
    half = HEAD_DIM // 2
    inv_freq = ROPE_THETA ** (-jnp.arange(half, dtype=F32) / half)
    ang = pos.astype(F32)[:, None] * inv_freq[None, :]
    cos, sin = jnp.cos(ang), jnp.sin(ang)
    return jnp.concatenate([cos, cos, -sin, sin], axis=-1)


def kernel(x_prompt, x_sample, c_prompt, c_sample, cache_k, cache_v, page_table, norm_g, w_ada, b_ada,
           ffn_w_in, ffn_w_out, attn_w_qkv, attn_w_o, attn_lambda, attn_subln_g, sgu_w_in, sgu_ln_g,
           sgu_ln_b, sgu_w_s, sgu_b_s, sgu_w_out, final_g):
    nb, t, d = x_prompt.shape
    db, ts, _ = x_sample.shape
    depth = w_ada.shape[0]
    n_pages = page_table.shape[1]
    past = n_pages * PAGE_SIZE
    mp, ms = nb * t, db * ts
    tm = min(ROW_TILE, t)
    tps = t // tm

    c_all = jnp.concatenate([c_prompt, c_sample, jnp.zeros((N_SEQ_PAD - nb - db, d), F32)], axis=0)
    mod = _ada_call(c_all, w_ada, b_ada)
    mod_s = jnp.repeat(mod[:, :, nb:nb + db], ts, axis=2)

    cs_p = _rotary_table(jnp.arange(t, dtype=jnp.int32))
    cs_s = jnp.tile(_rotary_table(past + jnp.arange(ts, dtype=jnp.int32)), (db, 1))

    xp = x_prompt.reshape(mp, d)
    xs = x_sample.reshape(ms, d)
    prm = dict(tm=tm, per_row=False, tiles_per_seq=tps)
    srm = dict(tm=ms, per_row=True, tiles_per_seq=1)
    new_kp, new_vp, new_ks, new_vs, new_sgu_v = [], [], [], [], []

    for i in range(depth):
        last = i == depth - 1
        ffn = functools.partial(_ffn_call, norm_g=norm_g, w_in=ffn_w_in, w_out=ffn_w_out,
                                final_g=final_g, layer=i)
        xp = ffn(xp, mod, sub=0, which=0, final_norm=False, **prm)
        xs = ffn(xs, mod_s, sub=0, which=0, final_norm=False, **srm)
        if i % 2 == 0:
            a = i // 2
            lam_init = 0.8 - 0.6 * math.exp(-0.3 * i)
            qp, kp, kpb, vp, vpb = _qkv_call(xp, mod, norm_g, attn_w_qkv, cs_p, layer=i, a=a, **prm)
            op = _attn_call(qp, kpb, vpb, attn_lambda[a], attn_subln_g[a], n_seq=nb, seq_len=t,
                            lam_init=lam_init)
            xp = _proj_call(op, attn_w_o, xp, mod, layer=i, w_index=a, tn=512, **prm)
            qs, ks_, _, vs_, _ = _qkv_call(xs, mod_s, norm_g, attn_w_qkv, cs_s, layer=i, a=a, **srm)
            os_ = _decode_call(page_table, qs.astype(F32), ks_, vs_, attn_lambda[a], attn_subln_g[a],
                               cache_k, cache_v, a=a, lam_init=lam_init)
            xs = _proj_call(os_, attn_w_o, xs, mod_s, layer=i, w_index=a, tn=512, **srm)
            new_kp.append(kp)
            new_vp.append(vp)
            new_ks.append(ks_)
            new_vs.append(vs_)
        else:
            g = i // 2
            up, vp_, mean_p, rstd_p = _sgu_in_call(xp, mod, norm_g, sgu_w_in, layer=i, g=g, **prm)
            (gp,) = _sgu_gate_call(up, vp_, mean_p, rstd_p, sgu_ln_g, sgu_ln_b, sgu_w_s, sgu_b_s,
                                   g=g, tm=tm, chunk_rows=min(CHUNK, t), want_vn=False)
            xp = _proj_call(gp, sgu_w_out, xp, mod, layer=i, w_index=g, tn=256, **prm)
            us, vs2, mean_s, rstd_s = _sgu_in_call(xs, mod_s, norm_g, sgu_w_in, layer=i, g=g, **srm)
            gs, vn_s = _sgu_gate_call(us, vs2, mean_s, rstd_s, sgu_ln_g, sgu_ln_b, sgu_w_s, sgu_b_s,
                                      g=g, tm=ms, chunk_rows=ts, want_vn=True)
            xs = _proj_call(gs, sgu_w_out, xs, mod_s, layer=i, w_index=g, tn=256, **srm)
            new_sgu_v.append(vn_s)
        xp = ffn(xp, mod, sub=2, which=1, final_norm=last, **prm)
        xs = ffn(xs, mod_s, sub=2, which=1, final_norm=last, **srm)

    n_sub = 2 * (d // V_DIM)
    y_prompt = xp.reshape(nb, t, d)
    y_sample = xs.reshape(db, ts, d)
    k_prompt = jnp.stack(new_kp).reshape(-1, nb, t, n_sub, HEAD_DIM)
    v_prompt = jnp.stack(new_vp).reshape(-1, nb, t, n_sub // 2, V_DIM)
    k_sample = jnp.stack(new_ks).reshape(-1, db, ts, n_sub, HEAD_DIM)
    v_sample = jnp.stack(new_vs).reshape(-1, db, ts, n_sub // 2, V_DIM)
    sgu_v_sample = jnp.stack(new_sgu_v).reshape(len(new_sgu_v), db, ts, -1)
    return (y_prompt, y_sample, k_prompt, v_prompt, k_sample, v_sample, sgu_v_sample)
```

```python
import functools
import math

import jax
import jax.numpy as jnp
from jax import lax
from jax.experimental import pallas as pl
from jax.experimental.pallas import tpu as pltpu

F32 = jnp.float32
BF16 = jnp.bfloat16

EPS = 1e-6
ROPE_THETA = 10000.0
HEAD_DIM = 128
V_DIM = 2 * HEAD_DIM
QK_SCALE = HEAD_DIM ** -0.5
CHUNK = 128
SGU_GROUPS = 8
PAGE_SIZE = 128
N_MOD = 9
N_SEQ_PAD = 16
BIG = 1e30

VMEM_LIMIT_BYTES = 60 * 1024 * 1024
ROW_TILE = 1024
ATT_BLOCK = 256
ATT_QUERY_BLOCK = 512
PAGES_PER_STEP = 4


def _cparams(sem):
    return pltpu.CompilerParams(dimension_semantics=sem, vmem_limit_bytes=VMEM_LIMIT_BYTES)


def _dot(a, b):
    return jnp.dot(a, b, preferred_element_type=F32)


def _dot_nt(a, b):
    return lax.dot_general(a, b, (((1,), (1,)), ((), ())), preferred_element_type=F32)


def _dot_tn(a, b):
    return lax.dot_general(a, b, (((0,), (0,)), ((), ())), preferred_element_type=F32)


def _rms(x, g):
    return x * lax.rsqrt(jnp.mean(x * x, axis=-1, keepdims=True) + EPS) * g


def _mod_row(mod_ref, k, seq, per_row):
    if per_row:
        return mod_ref[k]
    return mod_ref[k, pl.ds(seq, 1), :]


def _ada_kernel(c_ref, w_ref, b_ref, o_ref):
    c = c_ref[...]
    a = (c * jax.nn.sigmoid(c)).astype(BF16)
    o_ref[...] = _dot(a, w_ref[...].astype(BF16)) + b_ref[...]


def _ada_call(c_all, w_ada, b_ada):
    depth, d, _ = w_ada.shape
    tn = 1024
    nt = d // tn
    b4 = b_ada.reshape(depth, N_MOD, 1, d)
    return pl.pallas_call(
        _ada_kernel,
        grid=(depth, N_MOD * nt),
        in_specs=[
            pl.BlockSpec((N_SEQ_PAD, d), lambda l, n: (0, 0)),
            pl.BlockSpec((None, d, tn), lambda l, n: (l, 0, n)),
            pl.BlockSpec((None, None, 1, tn), lambda l, n: (l, n // nt, 0, n % nt)),
        ],
        out_specs=pl.BlockSpec((None, None, N_SEQ_PAD, tn), lambda l, n: (l, n // nt, 0, n % nt)),
        out_shape=jax.ShapeDtypeStruct((depth, N_MOD, N_SEQ_PAD, d), F32),
        compiler_params=_cparams(("arbitrary", "arbitrary")),
        name="adaln",
    )(c_all, w_ada, b4)


def _ffn_kernel(x_ref, mod_ref, g_ref, wg_ref, wu_ref, wo_ref, fg_ref, o_ref, h_ref, *,
                per_row, tiles_per_seq, final_norm, row_chunk):
    i = pl.program_id(0)
    j = pl.program_id(1)
    nj = pl.num_programs(1)
    seq = i // tiles_per_seq

    @pl.when(j == 0)
    def _():
        shift = _mod_row(mod_ref, 0, seq, per_row)
        scale = _mod_row(mod_ref, 1, seq, per_row)
        h = _rms(x_ref[...], g_ref[...]) * (1.0 + scale) + shift
        h_ref[...] = h.astype(BF16)
        o_ref[...] = jnp.zeros(o_ref.shape, F32)

    wg = wg_ref[...]
    wu = wu_ref[...]
    wo = wo_ref[...]
    for r in range(0, h_ref.shape[0], row_chunk):
        rows = slice(r, r + row_chunk)
        h = h_ref[rows, :]
        gate = _dot(h, wg)
        up = _dot(h, wu)
        a = (gate * jax.nn.sigmoid(gate) * up).astype(BF16)
        o_ref[rows, :] += _dot(a, wo)

    @pl.when(j == nj - 1)
    def _():
        gmod = _mod_row(mod_ref, 2, seq, per_row)
        y = x_ref[...] + 0.5 * gmod * o_ref[...]
        if final_norm:
            y = _rms(y, fg_ref[...])
        o_ref[...] = y


def _ffn_call(x, mod, norm_g, w_in, w_out, final_g, *, layer, sub, which, tm, per_row,
              tiles_per_seq, final_norm):
    m, d = x.shape
    d_ff = w_out.shape[2]
    tf = 512
    nj = d_ff // tf
    rows = mod.shape[2]
    g3 = norm_g.reshape(norm_g.shape[0], norm_g.shape[1], 1, d)
    kern = functools.partial(_ffn_kernel, per_row=per_row, tiles_per_seq=tiles_per_seq,
                             final_norm=final_norm, row_chunk=min(tm, 256))
    return pl.pallas_call(
        kern,
        grid=(m // tm, nj),
        in_specs=[
            pl.BlockSpec((tm, d), lambda i, j: (i, 0), pipeline_mode=pl.Buffered(1)),
            pl.BlockSpec((None, 3, rows, d), lambda i, j: (layer, sub, 0, 0)),
            pl.BlockSpec((None, None, 1, d), lambda i, j: (layer, sub, 0, 0)),
            pl.BlockSpec((None, None, d, tf), lambda i, j: (layer, which, 0, j)),
            pl.BlockSpec((None, None, d, tf), lambda i, j: (layer, which, 0, nj + j)),
            pl.BlockSpec((None, None, tf, d), lambda i, j: (layer, which, j, 0)),
            pl.BlockSpec((1, d), lambda i, j: (0, 0)),
        ],
        out_specs=pl.BlockSpec((tm, d), lambda i, j: (i, 0)),
        out_shape=jax.ShapeDtypeStruct((m, d), F32),
        scratch_shapes=[pltpu.VMEM((tm, d), BF16)],
        compiler_params=_cparams(("arbitrary", "arbitrary")),
        name="ffn_half_step",
    )(x, mod, g3, w_in, w_in, w_out, final_g.reshape(1, d))


def _qkv_kernel(x_ref, mod_ref, g_ref, w_ref, cs_ref, q_ref, k_ref, v_ref, *rest,
                per_row, tiles_per_seq, tn, prompt):
    if prompt:
        kb_ref, vt_ref, h_ref = rest
    else:
        (h_ref,) = rest
    i = pl.program_id(0)
    p = pl.program_id(1)
    n = pl.program_id(2)
    seq = i // tiles_per_seq

    @pl.when((p == 0) & (n == 0))
    def _():
        shift = _mod_row(mod_ref, 0, seq, per_row)
        scale = _mod_row(mod_ref, 1, seq, per_row)
        h = _rms(x_ref[...], g_ref[...]) * (1.0 + scale) + shift
        h_ref[...] = h.astype(BF16)

    acc = _dot(h_ref[...], w_ref[...])

    def rotary(dst_refs, mult):
        cos2 = cs_ref[:, :HEAD_DIM]
        sin2 = cs_ref[:, HEAD_DIM:]
        for hh in range(tn // HEAD_DIM):
            sl = slice(hh * HEAD_DIM, (hh + 1) * HEAD_DIM)
            xh = acc[:, sl]
            r = xh * cos2 + pltpu.roll(xh, HEAD_DIM // 2, 1) * sin2
            if mult is not None:
                r = r * mult
            for ref in dst_refs:
                ref[:, sl] = r.astype(ref.dtype)

    @pl.when(p == 0)
    def _():
        rotary([q_ref], QK_SCALE)

    @pl.when(p == 1)
    def _():
        rotary([k_ref, kb_ref] if prompt else [k_ref], None)

    @pl.when(p == 2)
    def _():
        v_ref[...] = acc
        if prompt:
            for r in range(vt_ref.shape[0]):
                vt_ref[r] = acc[r * ATT_BLOCK:(r + 1) * ATT_BLOCK, :].T.astype(BF16)


def _qkv_call(x, mod, norm_g, w_qkv, cs, *, layer, a, tm, per_row, tiles_per_seq, prompt):
    m, d = x.shape
    tn = 512
    nt = d // tn
    rows = mod.shape[2]
    g3 = norm_g.reshape(norm_g.shape[0], norm_g.shape[1], 1, d)
    n_cs_tiles = cs.shape[0] // tm
    kern = functools.partial(_qkv_kernel, per_row=per_row, tiles_per_seq=tiles_per_seq, tn=tn,
                             prompt=prompt)

    def part_col(part, p, n):
        return jnp.where(p < part, 0, jnp.where(p == part, n, nt - 1))

    def out_block(part):
        return pl.BlockSpec((tm, tn), lambda i, p, n: (i, part_col(part, p, n)))

    out_specs = [out_block(0), out_block(1), out_block(2)]
    out_shape = [
        jax.ShapeDtypeStruct((m, d), BF16 if prompt else F32),
        jax.ShapeDtypeStruct((m, d), F32),
        jax.ShapeDtypeStruct((m, d), F32),
    ]
    if prompt:
        nb = tm // ATT_BLOCK
        out_specs += [out_block(1),
                      pl.BlockSpec((nb, tn, ATT_BLOCK), lambda i, p, n: (i, part_col(2, p, n), 0))]
        out_shape += [jax.ShapeDtypeStruct((m, d), BF16),
                      jax.ShapeDtypeStruct((m // ATT_BLOCK, d, ATT_BLOCK), BF16)]
    return pl.pallas_call(
        kern,
        grid=(m // tm, 3, nt),
        in_specs=[
            pl.BlockSpec((tm, d), lambda i, p, n: (i, 0)),
            pl.BlockSpec((None, 3, rows, d), lambda i, p, n: (layer, 1, 0, 0)),
            pl.BlockSpec((None, None, 1, d), lambda i, p, n: (layer, 1, 0, 0)),
            pl.BlockSpec((None, d, tn), lambda i, p, n: (a, 0, p * nt + n)),
            pl.BlockSpec((tm, 2 * HEAD_DIM), lambda i, p, n: (i % n_cs_tiles, 0)),
        ],
        out_specs=out_specs,
        out_shape=out_shape,
        scratch_shapes=[pltpu.VMEM((tm, d), BF16)],
        compiler_params=_cparams(("arbitrary", "arbitrary", "arbitrary")),
        name="attn_qkv",
    )(x, mod, g3, w_qkv, cs)


def _lambda_full(lam_ref, lam_init):
    lv = lam_ref[...]
    d1 = jnp.sum(lv[0:1] * lv[1:2], axis=-1, keepdims=True)
    d2 = jnp.sum(lv[2:3] * lv[3:4], axis=-1, keepdims=True)
    return jnp.exp(d1) - jnp.exp(d2) + lam_init


def _attn_kernel(q_ref, k_ref, vt_ref, lam_ref, sg_ref, o_ref, acc_ref, m_ref, l_ref, *,
                 tq, blk, lam_init):
    qi = pl.program_id(2)
    kq = tq // blk
    q = q_ref[...]
    q1 = q[:, :HEAD_DIM]
    q2 = q[:, HEAD_DIM:]
    acc_ref[...] = jnp.zeros(acc_ref.shape, F32)
    m_ref[...] = jnp.full(m_ref.shape, -BIG, F32)
    l_ref[...] = jnp.zeros(l_ref.shape, F32)

    def block(j, key_offset):
        start = pl.multiple_of(j * blk, blk)
        kb = k_ref[pl.ds(start, blk), :]
        s = jnp.concatenate([_dot_nt(kb[:, :HEAD_DIM], q1), _dot_nt(kb[:, HEAD_DIM:], q2)], axis=1)
        if key_offset is not None:
            key = key_offset + lax.broadcasted_iota(jnp.int32, s.shape, 0)
            qry = lax.broadcasted_iota(jnp.int32, s.shape, 1) % tq
            s = jnp.where(key <= qry, s, -BIG)
        m_old = m_ref[...]
        m_new = jnp.maximum(m_old, jnp.max(s, axis=0, keepdims=True))
        p = jnp.exp(s - m_new)
        alpha = jnp.exp(m_old - m_new)
        m_ref[...] = m_new
        l_ref[...] = alpha * l_ref[...] + jnp.sum(p, axis=0, keepdims=True)
        acc_ref[...] = alpha * acc_ref[...] + _dot(vt_ref[j], p.astype(BF16))

    def full_block(j, carry):
        block(j, None)
        return carry

    lax.fori_loop(0, qi * kq, full_block, 0)
    for dj in range(kq):
        block(qi * kq + dj, dj * blk)

    lam = _lambda_full(lam_ref, lam_init)
    on = acc_ref[...] / l_ref[...]
    o = on[:, :tq] - lam * on[:, tq:]
    o = o * lax.rsqrt(jnp.mean(o * o, axis=0, keepdims=True) + EPS) * sg_ref[...]
    o_ref[...] = (o * (1.0 - lam_init)).T.astype(o_ref.dtype)


def _attn_call(q, kb, vt, lam_vecs, subln_g, *, n_seq, seq_len, lam_init):
    m, d = q.shape
    n_heads = d // V_DIM
    blk = ATT_BLOCK
    tq = min(ATT_QUERY_BLOCK, seq_len)
    nq = seq_len // tq
    kern = functools.partial(_attn_kernel, tq=tq, blk=blk, lam_init=lam_init)
    return pl.pallas_call(
        kern,
        grid=(n_seq, n_heads, nq),
        in_specs=[
            pl.BlockSpec((tq, V_DIM), lambda b, h, qi: (b * nq + qi, h)),
            pl.BlockSpec((seq_len, V_DIM), lambda b, h, qi: (b, h)),
            pl.BlockSpec((seq_len // blk, V_DIM, blk), lambda b, h, qi: (b, h, 0)),
            pl.BlockSpec((4, HEAD_DIM), lambda b, h, qi: (0, 0)),
            pl.BlockSpec((V_DIM, 1), lambda b, h, qi: (0, 0)),
        ],
        out_specs=pl.BlockSpec((tq, V_DIM), lambda b, h, qi: (b * nq + qi, h)),
        out_shape=jax.ShapeDtypeStruct((m, d), BF16),
        scratch_shapes=[pltpu.VMEM((V_DIM, 2 * tq), F32), pltpu.VMEM((1, 2 * tq), F32),
                        pltpu.VMEM((1, 2 * tq), F32)],
        compiler_params=_cparams(("arbitrary", "arbitrary", "arbitrary")),
        name="prompt_diff_attn",
    )(q, kb, vt, lam_vecs, subln_g.reshape(V_DIM, 1))


def _decode_kernel(pt_ref, qt_ref, kn_ref, vn_ref, lam_ref, sg_ref, *rest, n_heads, ts, lam_init):
    del pt_ref
    npg = PAGES_PER_STEP
    k_refs = rest[:npg]
    v_refs = rest[npg:2 * npg]
    o_ref, m_ref, l_ref, acc_ref = rest[2 * npg:2 * npg + 4]
    p_refs = rest[2 * npg + 4:]
    c = pl.program_id(1)
    nc = pl.num_programs(1)
    n_sub = 2 * n_heads
    cols = n_sub * ts
    keys = PAGE_SIZE

    hrow = lax.broadcasted_iota(jnp.int32, (n_sub, cols), 0)
    hlane = lax.broadcasted_iota(jnp.int32, (n_sub, cols), 1) // ts
    valid = hrow == hlane

    def to_col(a):
        row = jnp.sum(jnp.where(valid, a, 0.0), axis=0, keepdims=True)
        ri = lax.broadcasted_iota(jnp.int32, (cols, cols), 0)
        ci = lax.broadcasted_iota(jnp.int32, (cols, cols), 1)
        return jnp.sum(jnp.where(ri == ci, jnp.broadcast_to(row, (cols, cols)), 0.0), axis=1, keepdims=True)

    @pl.when(c == 0)
    def _():
        m_ref[...] = jnp.where(valid, -BIG, BIG)
        l_ref[...] = jnp.zeros(l_ref.shape, F32)
        acc_ref[...] = jnp.zeros(acc_ref.shape, F32)

    qt2 = qt_ref[...].astype(BF16)

    def flat_k(ref):
        return ref[...].reshape(keys * n_sub, HEAD_DIM).astype(BF16)

    def flat_v(ref):
        return ref[...].reshape(keys * n_heads, V_DIM).astype(BF16)

    def accumulate(r_list, v_list):
        m_old = m_ref[...]
        m_new = m_old
        for r in r_list:
            m_new = jnp.maximum(m_new, jnp.max(r, axis=0))
        alpha = jnp.exp(m_old - m_new)
        l_new = alpha * l_ref[...]
        o = None
        for u, (r, v) in enumerate(zip(r_list, v_list)):
            p = jnp.exp(r - m_new[None])
            l_new = l_new + jnp.sum(p, axis=0)
            p_refs[u][...] = p.reshape(keys * n_sub, cols)
            ps = (p_refs[u][pl.ds(0, keys * n_heads, stride=2), :]
                  + p_refs[u][pl.ds(1, keys * n_heads, stride=2), :]).astype(BF16)
            part = _dot_tn(ps, v)
            o = part if o is None else o + part
        m_ref[...] = m_new
        l_ref[...] = l_new
        acc_ref[...] = to_col(alpha) * acc_ref[...] + o

    r_list = []
    for u in range(0, npg, 2):
        k2 = jnp.concatenate([flat_k(k_refs[u]), flat_k(k_refs[u + 1])], axis=1)
        r2 = _dot(k2, qt2)
        r_list.append(r2[:, :cols].reshape(keys, n_sub, cols))
        r_list.append(r2[:, cols:].reshape(keys, n_sub, cols))
    accumulate(r_list, [flat_v(ref) for ref in v_refs])

    @pl.when(c == nc - 1)
    def _():
        r = _dot(kn_ref[...].astype(BF16), qt2[:HEAD_DIM, :cols]).reshape(keys, n_sub, cols)
        key = lax.broadcasted_iota(jnp.int32, r.shape, 0)
        qry = lax.broadcasted_iota(jnp.int32, r.shape, 2) % ts
        r = jnp.where(key <= qry, r, -BIG)
        accumulate([r], [vn_ref[...].astype(BF16)])
        lam = _lambda_full(lam_ref, lam_init)
        on = acc_ref[...] / to_col(l_ref[...])
        for hh in range(n_heads):
            o1 = on[(2 * hh) * ts:(2 * hh + 1) * ts]
            o2 = on[(2 * hh + 1) * ts:(2 * hh + 2) * ts]
            o = o1 - lam * o2
            o_ref[:, hh * V_DIM:(hh + 1) * V_DIM] = _rms(o, sg_ref[...]) * (1.0 - lam_init)


def _decode_call(page_table, q, k_new, v_new, lam_vecs, subln_g, cache_k, cache_v, *, a, lam_init):
    n_seq, n_pages = page_table.shape
    m, d = q.shape
    ts = m // n_seq
    n_heads = d // V_DIM
    n_sub = 2 * n_heads
    cols = n_sub * ts
    npg = PAGES_PER_STEP
    nc = n_pages // npg
    pad = PAGE_SIZE - ts
    qt = jnp.transpose(q.reshape(n_seq, ts, n_sub, HEAD_DIM), (0, 3, 2, 1)).reshape(n_seq, HEAD_DIM, cols)
    z = jnp.zeros_like(qt)
    qt2 = jnp.concatenate([jnp.concatenate([qt, z], axis=2), jnp.concatenate([z, qt], axis=2)], axis=1)
    kn = jnp.pad(k_new.reshape(n_seq, ts, n_sub, HEAD_DIM), ((0, 0), (0, pad), (0, 0), (0, 0)))
    vn = jnp.pad(v_new.reshape(n_seq, ts, n_heads, V_DIM), ((0, 0), (0, pad), (0, 0), (0, 0)))
    kn = kn.reshape(n_seq, PAGE_SIZE * n_sub, HEAD_DIM)
    vn = vn.reshape(n_seq, PAGE_SIZE * n_heads, V_DIM)

    def page_spec(shape, u):
        return pl.BlockSpec((None, None) + shape,
                            lambda b, c, pt: (a, pt[b, c * npg + u], 0, 0, 0))

    k_specs = [page_spec(cache_k.shape[2:], u) for u in range(npg)]
    v_specs = [page_spec(cache_v.shape[2:], u) for u in range(npg)]
    kern = functools.partial(_decode_kernel, n_heads=n_heads, ts=ts, lam_init=lam_init)
    grid_spec = pltpu.PrefetchScalarGridSpec(
        num_scalar_prefetch=1,
        grid=(n_seq, nc),
        in_specs=[
            pl.BlockSpec((None, 2 * HEAD_DIM, 2 * cols), lambda b, c, pt: (b, 0, 0)),
            pl.BlockSpec((None, PAGE_SIZE * n_sub, HEAD_DIM), lambda b, c, pt: (b, 0, 0)),
            pl.BlockSpec((None, PAGE_SIZE * n_heads, V_DIM), lambda b, c, pt: (b, 0, 0)),
            pl.BlockSpec((4, HEAD_DIM), lambda b, c, pt: (0, 0)),
            pl.BlockSpec((1, V_DIM), lambda b, c, pt: (0, 0)),
        ] + k_specs + v_specs,
        out_specs=pl.BlockSpec((ts, d), lambda b, c, pt: (b, 0)),
        scratch_shapes=[
            pltpu.VMEM((n_sub, cols), F32),
            pltpu.VMEM((n_sub, cols), F32),
            pltpu.VMEM((cols, V_DIM), F32),
        ] + [pltpu.VMEM((PAGE_SIZE * n_sub, cols), F32) for _ in range(npg)],
    )
    return pl.pallas_call(
        kern,
        grid_spec=grid_spec,
        out_shape=jax.ShapeDtypeStruct((m, d), F32),
        compiler_params=_cparams(("arbitrary", "arbitrary")),
        name="sample_paged_diff_attn",
    )(page_table, qt2, kn, vn, lam_vecs, subln_g.reshape(1, V_DIM),
      *([cache_k] * npg), *([cache_v] * npg))


def _proj_kernel(a_ref, w_ref, x_ref, mod_ref, o_ref, *, per_row, tiles_per_seq):
    seq = pl.program_id(0) // tiles_per_seq
    mix = _dot(a_ref[...].astype(BF16), w_ref[...])
    o_ref[...] = x_ref[...] + _mod_row(mod_ref, 2, seq, per_row) * mix


def _proj_call(act, w, x, mod, *, layer, w_index, tm, tn, per_row, tiles_per_seq):
    m, d = x.shape
    kdim = act.shape[1]
    rows = mod.shape[2]
    kern = functools.partial(_proj_kernel, per_row=per_row, tiles_per_seq=tiles_per_seq)
    return pl.pallas_call(
        kern,
        grid=(m // tm, d // tn),
        in_specs=[
            pl.BlockSpec((tm, kdim), lambda i, n: (i, 0)),
            pl.BlockSpec((None, kdim, tn), lambda i, n: (w_index, 0, n)),
            pl.BlockSpec((tm, tn), lambda i, n: (i, n)),
            pl.BlockSpec((None, 3, rows, tn), lambda i, n: (layer, 1, 0, n)),
        ],
        out_specs=pl.BlockSpec((tm, tn), lambda i, n: (i, n)),
        out_shape=jax.ShapeDtypeStruct((m, d), F32),
        compiler_params=_cparams(("arbitrary", "arbitrary")),
        name="gated_residual_proj",
    )(act, w, x, mod)


def _sgu_in_kernel(x_ref, mod_ref, g_ref, w_ref, u_ref, v_ref, mean_ref, rstd_ref,
                   h_ref, s1_ref, s2_ref, *, per_row, tiles_per_seq, half):
    i = pl.program_id(0)
    n = pl.program_id(1)
    nn = pl.num_programs(1)
    nu = nn // 2
    seq = i // tiles_per_seq

    @pl.when(n == 0)
    def _():
        shift = _mod_row(mod_ref, 0, seq, per_row)
        scale = _mod_row(mod_ref, 1, seq, per_row)
        h = _rms(x_ref[...], g_ref[...]) * (1.0 + scale) + shift
        h_ref[...] = h.astype(BF16)
        s1_ref[...] = jnp.zeros(s1_ref.shape, F32)
        s2_ref[...] = jnp.zeros(s2_ref.shape, F32)

    z = _dot(h_ref[...], w_ref[...])
    z = 0.5 * z * (1.0 + lax.erf(z * (2.0 ** -0.5)))

    @pl.when(n < nu)
    def _():
        u_ref[...] = z.astype(u_ref.dtype)

    @pl.when(n >= nu)
    def _():
        v_ref[...] = z
        s1_ref[...] += jnp.sum(z, axis=-1, keepdims=True)
        s2_ref[...] += jnp.sum(z * z, axis=-1, keepdims=True)

    @pl.when(n == nn - 1)
    def _():
        mean = s1_ref[...] * (1.0 / half)
        var = s2_ref[...] * (1.0 / half) - mean * mean
        mean_ref[...] = mean
        rstd_ref[...] = lax.rsqrt(var + EPS)


def _sgu_in_call(x, mod, norm_g, w_in, *, layer, g, tm, per_row, tiles_per_seq):
    m, d = x.shape
    ffn = w_in.shape[2]
    half = ffn // 2
    tn = 512
    nn = ffn // tn
    nu = nn // 2
    rows = mod.shape[2]
    g3 = norm_g.reshape(norm_g.shape[0], norm_g.shape[1], 1, d)
    kern = functools.partial(_sgu_in_kernel, per_row=per_row, tiles_per_seq=tiles_per_seq, half=half)
    return pl.pallas_call(
        kern,
        grid=(m // tm, nn),
        in_specs=[
            pl.BlockSpec((tm, d), lambda i, n: (i, 0)),
            pl.BlockSpec((None, 3, rows, d), lambda i, n: (layer, 1, 0, 0)),
            pl.BlockSpec((None, None, 1, d), lambda i, n: (layer, 1, 0, 0)),
            pl.BlockSpec((None, d, tn), lambda i, n: (g, 0, n)),
        ],
        out_specs=[
            pl.BlockSpec((tm, tn), lambda i, n: (i, jnp.minimum(n, nu - 1))),
            pl.BlockSpec((tm, tn), lambda i, n: (i, jnp.maximum(n - nu, 0))),
            pl.BlockSpec((tm, 1), lambda i, n: (i, 0)),
            pl.BlockSpec((tm, 1), lambda i, n: (i, 0)),
        ],
        out_shape=[
            jax.ShapeDtypeStruct((m, half), BF16),
            jax.ShapeDtypeStruct((m, half), F32),
            jax.ShapeDtypeStruct((m, 1), F32),
            jax.ShapeDtypeStruct((m, 1), F32),
        ],
        scratch_shapes=[pltpu.VMEM((tm, d), BF16), pltpu.VMEM((tm, 1), F32), pltpu.VMEM((tm, 1), F32)],
        compiler_params=_cparams(("arbitrary", "arbitrary")),
        name="sgu_in",
    )(x, mod, g3, w_in)


def _sgu_gate_kernel(u_ref, v_ref, mean_ref, rstd_ref, lg_ref, lb_ref, ws_ref, bs_ref,
                     o_ref, vn_ref, *, chunk_rows, n_chunks):
    vn = (v_ref[...] - mean_ref[...]) * rstd_ref[...] * lg_ref[...] + lb_ref[...]
    if vn_ref is not None:
        vn_ref[...] = vn
    vnb = vn.astype(BF16)
    cr = chunk_rows
    r = lax.broadcasted_iota(jnp.int32, (cr, cr), 0)
    c = lax.broadcasted_iota(jnp.int32, (cr, cr), 1)
    ws = jnp.where(c <= r, ws_ref[:cr, :cr], 0.0).astype(BF16)
    bs = bs_ref[:cr, :]
    for ch in range(n_chunks):
        sl = slice(ch * cr, (ch + 1) * cr)
        s = _dot(ws, vnb[sl]) + bs
        o_ref[sl, :] = (u_ref[sl, :].astype(F32) * s).astype(o_ref.dtype)


def _sgu_gate_call(u, v, mean, rstd, ln_g, ln_b, w_s, b_s, *, g, tm, chunk_rows, want_vn):
    m, half = u.shape
    gd = half // SGU_GROUPS
    n_chunks = tm // chunk_rows
    bs3 = b_s.reshape(b_s.shape[0], SGU_GROUPS, CHUNK, 1)
    lg3 = ln_g.reshape(ln_g.shape[0], 1, half)
    lb3 = ln_b.reshape(ln_b.shape[0], 1, half)

    def kern(*refs):
        if want_vn:
            _sgu_gate_kernel(*refs, chunk_rows=chunk_rows, n_chunks=n_chunks)
        else:
            _sgu_gate_kernel(*refs, None, chunk_rows=chunk_rows, n_chunks=n_chunks)

    blk = pl.BlockSpec((tm, gd), lambda i, j: (i, j))
    out_specs = [blk]
    out_shape = [jax.ShapeDtypeStruct((m, half), BF16)]
    if want_vn:
        out_specs.append(blk)
        out_shape.append(jax.ShapeDtypeStruct((m, half), F32))
    return pl.pallas_call(
        kern,
        grid=(m // tm, SGU_GROUPS),
        in_specs=[
            blk,
            blk,
            pl.BlockSpec((tm, 1), lambda i, j: (i, 0)),
            pl.BlockSpec((tm, 1), lambda i, j: (i, 0)),
            pl.BlockSpec((None, 1, gd), lambda i, j: (g, 0, j)),
            pl.BlockSpec((None, 1, gd), lambda i, j: (g, 0, j)),
            pl.BlockSpec((None, None, CHUNK, CHUNK), lambda i, j: (g, j, 0, 0)),
            pl.BlockSpec((None, None, CHUNK, 1), lambda i, j: (g, j, 0, 0)),
        ],
        out_specs=out_specs,
        out_shape=out_shape,
        compiler_params=_cparams(("arbitrary", "arbitrary")),
        name="sgu_gate",
    )(u, v, mean, rstd, lg3, lb3, w_s, bs3)


def _rotary_table(pos):
    half = HEAD_DIM // 2
    inv_freq = ROPE_THETA ** (-jnp.arange(half, dtype=F32) / half)
    ang = pos.astype(F32)[:, None] * inv_freq[None, :]
    cos, sin = jnp.cos(ang), jnp.sin(ang)
    return jnp.concatenate([cos, cos, -sin, sin], axis=-1)


def kernel(x_prompt, x_sample, c_prompt, c_sample, cache_k, cache_v, page_table, norm_g, w_ada, b_ada,
           ffn_w_in, ffn_w_out, attn_w_qkv, attn_w_o, attn_lambda, attn_subln_g, sgu_w_in, sgu_ln_g,
           sgu_ln_b, sgu_w_s, sgu_b_s, sgu_w_out, final_g):
    nb, t, d = x_prompt.shape
    db, ts, _ = x_sample.shape
    depth = w_ada.shape[0]
    n_pages = page_table.shape[1]
    past = n_pages * PAGE_SIZE
    mp, ms = nb * t, db * ts
    tm = min(ROW_TILE, t)
    tps = t // tm

    ffn_w_in, ffn_w_out, attn_w_qkv, attn_w_o, sgu_w_in, sgu_w_out = (
        w.astype(BF16) for w in (ffn_w_in, ffn_w_out, attn_w_qkv, attn_w_o, sgu_w_in, sgu_w_out))

    c_all = jnp.concatenate([c_prompt, c_sample, jnp.zeros((N_SEQ_PAD - nb - db, d), F32)], axis=0)
    mod = _ada_call(c_all, w_ada, b_ada)
    mod_s = jnp.repeat(mod[:, :, nb:nb + db], ts, axis=2)

    cs_p = _rotary_table(jnp.arange(t, dtype=jnp.int32))
    cs_s = jnp.tile(_rotary_table(past + jnp.arange(ts, dtype=jnp.int32)), (db, 1))

    xp = x_prompt.reshape(mp, d)
    xs = x_sample.reshape(ms, d)
    prm = dict(tm=tm, per_row=False, tiles_per_seq=tps)
    srm = dict(tm=ms, per_row=True, tiles_per_seq=1)
    new_kp, new_vp, new_ks, new_vs, new_sgu_v = [], [], [], [], []

    for i in range(depth):
        last = i == depth - 1
        ffn = functools.partial(_ffn_call, norm_g=norm_g, w_in=ffn_w_in, w_out=ffn_w_out,
                                final_g=final_g, layer=i)
        xp = ffn(xp, mod, sub=0, which=0, final_norm=False, **prm)
        xs = ffn(xs, mod_s, sub=0, which=0, final_norm=False, **srm)
        if i % 2 == 0:
            a = i // 2
            lam_init = 0.8 - 0.6 * math.exp(-0.3 * i)
            qp, kp, vp, kpb, vpt = _qkv_call(xp, mod, norm_g, attn_w_qkv, cs_p, layer=i, a=a,
                                             prompt=True, **prm)
            op = _attn_call(qp, kpb, vpt, attn_lambda[a], attn_subln_g[a], n_seq=nb, seq_len=t,
                            lam_init=lam_init)
            xp = _proj_call(op, attn_w_o, xp, mod, layer=i, w_index=a, tn=512, **prm)
            qs, ks_, vs_ = _qkv_call(xs, mod_s, norm_g, attn_w_qkv, cs_s, layer=i, a=a,
                                     prompt=False, **srm)
            os_ = _decode_call(page_table, qs, ks_, vs_, attn_lambda[a], attn_subln_g[a],
                               cache_k, cache_v, a=a, lam_init=lam_init)
            xs = _proj_call(os_, attn_w_o, xs, mod_s, layer=i, w_index=a, tn=512, **srm)
            new_kp.append(kp)
            new_vp.append(vp)
            new_ks.append(ks_)
            new_vs.append(vs_)
        else:
            g = i // 2
            up, vp_, mean_p, rstd_p = _sgu_in_call(xp, mod, norm_g, sgu_w_in, layer=i, g=g, **prm)
            (gp,) = _sgu_gate_call(up, vp_, mean_p, rstd_p, sgu_ln_g, sgu_ln_b, sgu_w_s, sgu_b_s,
                                   g=g, tm=tm, chunk_rows=min(CHUNK, t), want_vn=False)
            xp = _proj_call(gp, sgu_w_out, xp, mod, layer=i, w_index=g, tn=512, **prm)
            us, vs2, mean_s, rstd_s = _sgu_in_call(xs, mod_s, norm_g, sgu_w_in, layer=i, g=g, **srm)
            gs, vn_s = _sgu_gate_call(us, vs2, mean_s, rstd_s, sgu_ln_g, sgu_ln_b, sgu_w_s, sgu_b_s,
                                      g=g, tm=ms, chunk_rows=ts, want_vn=True)
            xs = _proj_call(gs, sgu_w_out, xs, mod_s, layer=i, w_index=g, tn=512, **srm)
            new_sgu_v.append(vn_s)
        xp = ffn(xp, mod, sub=2, which=1, final_norm=last, **prm)
        xs = ffn(xs, mod_s, sub=2, which=1, final_norm=last, **srm)

    n_sub = 2 * (d // V_DIM)
    y_prompt = xp.reshape(nb, t, d)
    y_sample = xs.reshape(db, ts, d)
    k_prompt = jnp.stack(new_kp).reshape(-1, nb, t, n_sub, HEAD_DIM)
    v_prompt = jnp.stack(new_vp).reshape(-1, nb, t, n_sub // 2, V_DIM)
    k_sample = jnp.stack(new_ks).reshape(-1, db, ts, n_sub, HEAD_DIM)
    v_sample = jnp.stack(new_vs).reshape(-1, db, ts, n_sub // 2, V_DIM)
    sgu_v_sample = jnp.stack(new_sgu_v).reshape(len(new_sgu_v), db, ts, -1)
    return (y_prompt, y_sample, k_prompt, v_prompt, k_sample, v_sample, sgu_v_sample)
```

```python
import functools
import math

import jax
import jax.numpy as jnp
from jax import lax
from jax.experimental import pallas as pl
from jax.experimental.pallas import tpu as pltpu

F32 = jnp.float32
BF16 = jnp.bfloat16

EPS = 1e-6
ROPE_THETA = 10000.0
HEAD_DIM = 128
V_DIM = 2 * HEAD_DIM
Q_SCALE = HEAD_DIM ** -0.5 * math.log2(math.e)
CHUNK = 128
SGU_GROUPS = 8
PAGE_SIZE = 128
N_MOD = 9
N_SEQ_PAD = 16
BIG = 1e30

VMEM_LIMIT_BYTES = 60 * 1024 * 1024
ROW_TILE = 1024
ROW_CHUNK = 256
ATT_BLOCK = 256
ATT_QUERY_BLOCK = 512
PAGES_PER_STEP = 8


def _cparams(sem):
    return pltpu.CompilerParams(dimension_semantics=sem, vmem_limit_bytes=VMEM_LIMIT_BYTES)


def _dot(a, b):
    return jnp.dot(a, b, preferred_element_type=F32)


def _dot_nt(a, b):
    return lax.dot_general(a, b, (((1,), (1,)), ((), ())), preferred_element_type=F32)


def _dot_tn(a, b):
    return lax.dot_general(a, b, (((0,), (0,)), ((), ())), preferred_element_type=F32)


def _rms(x, g):
    return x * lax.rsqrt(jnp.mean(x * x, axis=-1, keepdims=True) + EPS) * g


def _mod_rows(mod_ref, k, seq, per_row, rows):
    if per_row:
        return mod_ref[k, rows, :]
    return mod_ref[k, pl.ds(seq, 1), :]


def _modulated(x_ref, mod_ref, g_ref, seq, per_row, rows):
    shift = _mod_rows(mod_ref, 0, seq, per_row, rows)
    scale = _mod_rows(mod_ref, 1, seq, per_row, rows)
    return (_rms(x_ref[rows, :], g_ref[...]) * (1.0 + scale) + shift).astype(BF16)


def _ada_kernel(c_ref, w_ref, b_ref, o_ref):
    c = c_ref[...]
    a = (c * jax.nn.sigmoid(c)).astype(BF16)
    o_ref[...] = _dot(a, w_ref[...].astype(BF16)) + b_ref[...]


def _ada_call(c_all, w_ada, b_ada):
    depth, d, _ = w_ada.shape
    tn = 1024
    nt = d // tn
    b4 = b_ada.reshape(depth, N_MOD, 1, d)
    return pl.pallas_call(
        _ada_kernel,
        grid=(depth, N_MOD * nt),
        in_specs=[
            pl.BlockSpec((N_SEQ_PAD, d), lambda l, n: (0, 0)),
            pl.BlockSpec((None, d, tn), lambda l, n: (l, 0, n)),
            pl.BlockSpec((None, None, 1, tn), lambda l, n: (l, n // nt, 0, n % nt)),
        ],
        out_specs=pl.BlockSpec((None, None, N_SEQ_PAD, tn), lambda l, n: (l, n // nt, 0, n % nt)),
        out_shape=jax.ShapeDtypeStruct((depth, N_MOD, N_SEQ_PAD, d), F32),
        compiler_params=_cparams(("arbitrary", "arbitrary")),
        name="adaln",
    )(c_all, w_ada, b4)


def _ffn_kernel(x_ref, mod_ref, g_ref, wg_ref, wu_ref, wo_ref, fg_ref, o_ref, h_ref, *,
                per_row, tiles_per_seq, final_norm, row_chunk):
    i = pl.program_id(0)
    j = pl.program_id(1)
    nj = pl.num_programs(1)
    seq = i // tiles_per_seq
    wg = wg_ref[...]
    wu = wu_ref[...]
    wo = wo_ref[...]

    def step(first, last):
        for r in range(0, h_ref.shape[0], row_chunk):
            rows = slice(r, r + row_chunk)
            if first:
                h = _modulated(x_ref, mod_ref, g_ref, seq, per_row, rows)
                h_ref[rows, :] = h
            else:
                h = h_ref[rows, :]
            gate = _dot(h, wg)
            up = _dot(h, wu)
            a = (gate * jax.nn.sigmoid(gate) * up).astype(BF16)
            tot = _dot(a, wo)
            if not first:
                tot = o_ref[rows, :] + tot
            if last:
                gmod = _mod_rows(mod_ref, 2, seq, per_row, rows)
                tot = x_ref[rows, :] + 0.5 * gmod * tot
                if final_norm:
                    tot = _rms(tot, fg_ref[...])
            o_ref[rows, :] = tot

    @pl.when(j == 0)
    def _():
        step(True, False)

    @pl.when((j > 0) & (j < nj - 1))
    def _():
        step(False, False)

    @pl.when(j == nj - 1)
    def _():
        step(False, True)


def _ffn_call(x, mod, norm_g, w_in, w_out, final_g, *, layer, sub, which, tm, per_row,
              tiles_per_seq, final_norm):
    m, d = x.shape
    d_ff = w_out.shape[2]
    tf = 512
    nj = d_ff // tf
    assert nj >= 2
    rows = mod.shape[2]
    g3 = norm_g.reshape(norm_g.shape[0], norm_g.shape[1], 1, d)
    kern = functools.partial(_ffn_kernel, per_row=per_row, tiles_per_seq=tiles_per_seq,
                             final_norm=final_norm, row_chunk=min(tm, ROW_CHUNK))
    return pl.pallas_call(
        kern,
        grid=(m // tm, nj),
        in_specs=[
            pl.BlockSpec((tm, d), lambda i, j: (i, 0), pipeline_mode=pl.Buffered(1)),
            pl.BlockSpec((None, 3, rows, d), lambda i, j: (layer, sub, 0, 0)),
            pl.BlockSpec((None, None, 1, d), lambda i, j: (layer, sub, 0, 0)),
            pl.BlockSpec((None, None, d, tf), lambda i, j: (layer, which, 0, j)),
            pl.BlockSpec((None, None, d, tf), lambda i, j: (layer, which, 0, nj + j)),
            pl.BlockSpec((None, None, tf, d), lambda i, j: (layer, which, j, 0)),
            pl.BlockSpec((1, d), lambda i, j: (0, 0)),
        ],
        out_specs=pl.BlockSpec((tm, d), lambda i, j: (i, 0)),
        out_shape=jax.ShapeDtypeStruct((m, d), F32),
        scratch_shapes=[pltpu.VMEM((tm, d), BF16)],
        compiler_params=_cparams(("arbitrary", "arbitrary")),
        name="ffn_half_step",
    )(x, mod, g3, w_in, w_in, w_out, final_g.reshape(1, d))


def _qkv_kernel(x_ref, mod_ref, g_ref, w_ref, cs_ref, *rest, per_row, tiles_per_seq, tn, prompt,
                aliased, row_chunk):
    if aliased:
        rest = rest[2:]
    if prompt:
        q_ref, k_ref, v_ref, kb_ref, vt_ref, h_ref = rest
    else:
        q_ref, k_ref, v_ref, h_ref = rest
    i = pl.program_id(0)
    p = pl.program_id(1)
    n = pl.program_id(2)
    seq = i // tiles_per_seq
    w = w_ref[...]

    def step(part, first):
        for ci, r in enumerate(range(0, h_ref.shape[0], row_chunk)):
            rows = slice(r, r + row_chunk)
            if first:
                h = _modulated(x_ref, mod_ref, g_ref, seq, per_row, rows)
                h_ref[rows, :] = h
            else:
                h = h_ref[rows, :]
            acc = _dot(h, w)
            if part == 2:
                v_ref[rows, :] = acc
                if prompt:
                    vt_ref[ci] = acc.T.astype(BF16)
                continue
            cos2 = cs_ref[rows, :HEAD_DIM]
            sin2 = cs_ref[rows, HEAD_DIM:]
            for hh in range(tn // HEAD_DIM):
                sl = slice(hh * HEAD_DIM, (hh + 1) * HEAD_DIM)
                xh = acc[:, sl]
                rot = xh * cos2 + pltpu.roll(xh, HEAD_DIM // 2, 1) * sin2
                if part == 0:
                    q_ref[rows, sl] = (rot * Q_SCALE).astype(q_ref.dtype)
                else:
                    k_ref[rows, sl] = rot
                    if prompt:
                        kb_ref[rows, sl] = rot.astype(BF16)

    @pl.when((p == 0) & (n == 0))
    def _():
        step(0, True)

    @pl.when((p == 0) & (n > 0))
    def _():
        step(0, False)

    @pl.when(p == 1)
    def _():
        step(1, False)

    @pl.when(p == 2)
    def _():
        step(2, False)


def _qkv_call(x, mod, norm_g, w_qkv, cs, kv_prev, *, layer, a, tm, per_row, tiles_per_seq, prompt):
    m, d = x.shape
    n_attn = w_qkv.shape[0]
    tn = 512
    nt = d // tn
    rows = mod.shape[2]
    g3 = norm_g.reshape(norm_g.shape[0], norm_g.shape[1], 1, d)
    n_cs_tiles = cs.shape[0] // tm
    row_chunk = min(tm, ATT_BLOCK)
    aliased = kv_prev is not None
    kern = functools.partial(_qkv_kernel, per_row=per_row, tiles_per_seq=tiles_per_seq, tn=tn,
                             prompt=prompt, aliased=aliased, row_chunk=row_chunk)

    def part_col(part, p, n):
        return jnp.where(p < part, 0, jnp.where(p == part, n, nt - 1))

    def out_block(part):
        return pl.BlockSpec((tm, tn), lambda i, p, n: (i, part_col(part, p, n)))

    def stacked_block(part):
        return pl.BlockSpec((None, tm, tn), lambda i, p, n: (a, i, part_col(part, p, n)))

    in_specs = [
        pl.BlockSpec((tm, d), lambda i, p, n: (i, 0)),
        pl.BlockSpec((None, 3, rows, d), lambda i, p, n: (layer, 1, 0, 0)),
        pl.BlockSpec((None, None, 1, d), lambda i, p, n: (layer, 1, 0, 0)),
        pl.BlockSpec((None, d, tn), lambda i, p, n: (a, 0, p * nt + n)),
        pl.BlockSpec((tm, 2 * HEAD_DIM), lambda i, p, n: (i % n_cs_tiles, 0)),
    ]
    args = [x, mod, g3, w_qkv, cs]
    aliases = {}
    if aliased:
        in_specs += [pl.BlockSpec(memory_space=pl.ANY), pl.BlockSpec(memory_space=pl.ANY)]
        args += list(kv_prev)
        aliases = {5: 1, 6: 2}
    out_specs = [out_block(0), stacked_block(1), stacked_block(2)]
    out_shape = [
        jax.ShapeDtypeStruct((m, d), BF16 if prompt else F32),
        jax.ShapeDtypeStruct((n_attn, m, d), F32),
        jax.ShapeDtypeStruct((n_attn, m, d), F32),
    ]
    if prompt:
        nb = tm // ATT_BLOCK
        out_specs += [out_block(1),
                      pl.BlockSpec((nb, tn, ATT_BLOCK), lambda i, p, n: (i, part_col(2, p, n), 0))]
        out_shape += [jax.ShapeDtypeStruct((m, d), BF16),
                      jax.ShapeDtypeStruct((m // ATT_BLOCK, d, ATT_BLOCK), BF16)]
    return pl.pallas_call(
        kern,
        grid=(m // tm, 3, nt),
        in_specs=in_specs,
        out_specs=out_specs,
        out_shape=out_shape,
        input_output_aliases=aliases,
        scratch_shapes=[pltpu.VMEM((tm, d), BF16)],
        compiler_params=_cparams(("arbitrary", "arbitrary", "arbitrary")),
        name="attn_qkv",
    )(*args)


def _lambda_full(lam_ref, lam_init):
    lv = lam_ref[...]
    d1 = jnp.sum(lv[0:1] * lv[1:2], axis=-1, keepdims=True)
    d2 = jnp.sum(lv[2:3] * lv[3:4], axis=-1, keepdims=True)
    return jnp.exp(d1) - jnp.exp(d2) + lam_init


def _attn_kernel(q_ref, k_ref, vt_ref, lam_ref, sg_ref, o_ref, acc_ref, m_ref, l_ref, *,
                 tq, blk, lam_init):
    qi = pl.program_id(2)
    kq = tq // blk
    q = q_ref[...]
    q1 = q[:, :HEAD_DIM]
    q2 = q[:, HEAD_DIM:]
    acc_ref[...] = jnp.zeros(acc_ref.shape, F32)
    m_ref[...] = jnp.full(m_ref.shape, -BIG, F32)
    l_ref[...] = jnp.zeros(l_ref.shape, F32)

    def update(kb, vt, key_offset):
        s = jnp.concatenate([_dot_nt(kb[:, :HEAD_DIM], q1), _dot_nt(kb[:, HEAD_DIM:], q2)], axis=1)
        if key_offset is not None:
            key = key_offset + lax.broadcasted_iota(jnp.int32, s.shape, 0)
            qry = lax.broadcasted_iota(jnp.int32, s.shape, 1) % tq
            s = jnp.where(key <= qry, s, -BIG)
        m_old = m_ref[...]
        m_new = jnp.maximum(m_old, jnp.max(s, axis=0, keepdims=True))
        p = jnp.exp2(s - m_new)
        alpha = jnp.exp2(m_old - m_new)
        m_ref[...] = m_new
        l_ref[...] = alpha * l_ref[...] + jnp.sum(p, axis=0, keepdims=True)
        acc_ref[...] = alpha * acc_ref[...] + _dot(vt, p.astype(BF16))

    def earlier_keys(j, carry):
        start = pl.multiple_of(j * tq, tq)
        vt = jnp.concatenate([vt_ref[j * kq + d] for d in range(kq)], axis=1)
        update(k_ref[pl.ds(start, tq), :], vt, None)
        return carry

    lax.fori_loop(0, qi, earlier_keys, 0)
    for dj in range(kq):
        start = pl.multiple_of(qi * tq + dj * blk, blk)
        update(k_ref[pl.ds(start, blk), :], vt_ref[qi * kq + dj], dj * blk)

    lam = _lambda_full(lam_ref, lam_init)
    on = acc_ref[...] / l_ref[...]
    o = on[:, :tq] - lam * on[:, tq:]
    o = o * lax.rsqrt(jnp.mean(o * o, axis=0, keepdims=True) + EPS) * sg_ref[...]
    o_ref[...] = (o * (1.0 - lam_init)).T.astype(o_ref.dtype)


def _attn_call(q, kb, vt, lam_vecs, subln_g, *, n_seq, seq_len, lam_init):
    m, d = q.shape
    n_heads = d // V_DIM
    blk = ATT_BLOCK
    tq = min(ATT_QUERY_BLOCK, seq_len)
    nq = seq_len // tq
    kern = functools.partial(_attn_kernel, tq=tq, blk=blk, lam_init=lam_init)
    return pl.pallas_call(
        kern,
        grid=(n_seq, n_heads, nq),
        in_specs=[
            pl.BlockSpec((tq, V_DIM), lambda b, h, qi: (b * nq + qi, h)),
            pl.BlockSpec((seq_len, V_DIM), lambda b, h, qi: (b, h)),
            pl.BlockSpec((seq_len // blk, V_DIM, blk), lambda b, h, qi: (b, h, 0)),
            pl.BlockSpec((4, HEAD_DIM), lambda b, h, qi: (0, 0)),
            pl.BlockSpec((V_DIM, 1), lambda b, h, qi: (0, 0)),
        ],
        out_specs=pl.BlockSpec((tq, V_DIM), lambda b, h, qi: (b * nq + qi, h)),
        out_shape=jax.ShapeDtypeStruct((m, d), BF16),
        scratch_shapes=[pltpu.VMEM((V_DIM, 2 * tq), F32), pltpu.VMEM((1, 2 * tq), F32),
                        pltpu.VMEM((1, 2 * tq), F32)],
        compiler_params=_cparams(("arbitrary", "arbitrary", "arbitrary")),
        name="prompt_diff_attn",
    )(q, kb, vt, lam_vecs, subln_g.reshape(V_DIM, 1))


def _decode_kernel(pt_ref, qt_ref, kn_ref, vn_ref, lam_ref, sg_ref, *rest, n_heads, ts, lam_init):
    del pt_ref
    npg = PAGES_PER_STEP
    k_refs = rest[:npg]
    v_refs = rest[npg:2 * npg]
    o_ref, m_ref, l_ref, acc_ref = rest[2 * npg:2 * npg + 4]
    p_refs = rest[2 * npg + 4:]
    c = pl.program_id(1)
    nc = pl.num_programs(1)
    n_sub = 2 * n_heads
    cols = n_sub * ts
    keys = PAGE_SIZE

    hrow = lax.broadcasted_iota(jnp.int32, (n_sub, cols), 0)
    hlane = lax.broadcasted_iota(jnp.int32, (n_sub, cols), 1) // ts
    valid = hrow == hlane

    def to_col(a):
        row = jnp.sum(jnp.where(valid, a, 0.0), axis=0, keepdims=True)
        ri = lax.broadcasted_iota(jnp.int32, (cols, cols), 0)
        ci = lax.broadcasted_iota(jnp.int32, (cols, cols), 1)
        return jnp.sum(jnp.where(ri == ci, jnp.broadcast_to(row, (cols, cols)), 0.0), axis=1, keepdims=True)

    @pl.when(c == 0)
    def _():
        m_ref[...] = jnp.where(valid, -BIG, BIG)
        l_ref[...] = jnp.zeros(l_ref.shape, F32)
        acc_ref[...] = jnp.zeros(acc_ref.shape, F32)

    qt2 = qt_ref[...].astype(BF16)

    def flat_k(ref):
        return ref[...].reshape(keys * n_sub, HEAD_DIM).astype(BF16)

    def flat_v(ref):
        return ref[...].reshape(keys * n_heads, V_DIM).astype(BF16)

    def accumulate(state, r_list, v_list, scratch):
        m_old, l_old, acc = state
        m_new = m_old
        for r in r_list:
            m_new = jnp.maximum(m_new, jnp.max(r, axis=0))
        alpha = jnp.exp2(m_old - m_new)
        l_new = alpha * l_old
        o = None
        for r, v, p_ref in zip(r_list, v_list, scratch):
            p = jnp.exp2(r - m_new[None])
            l_new = l_new + jnp.sum(p, axis=0)
            p_ref[...] = p.reshape(keys * n_sub, cols)
            ps = (p_ref[pl.ds(0, keys * n_heads, stride=2), :]
                  + p_ref[pl.ds(1, keys * n_heads, stride=2), :]).astype(BF16)
            part = _dot_tn(ps, v)
            o = part if o is None else o + part
        return m_new, l_new, to_col(alpha) * acc + o

    def pair_scores(u):
        k2 = jnp.concatenate([flat_k(k_refs[u]), flat_k(k_refs[u + 1])], axis=1)
        r2 = _dot(k2, qt2)
        return [r2[:, :cols].reshape(keys, n_sub, cols), r2[:, cols:].reshape(keys, n_sub, cols)]

    state = (m_ref[...], l_ref[...], acc_ref[...])
    r_next = pair_scores(0)
    for u in range(0, npg, 2):
        r_pair = r_next
        if u + 2 < npg:
            r_next = pair_scores(u + 2)
        slot = u % 4
        state = accumulate(state, r_pair, [flat_v(v_refs[u]), flat_v(v_refs[u + 1])],
                           p_refs[slot:slot + 2])
    m_ref[...], l_ref[...], acc_ref[...] = state

    @pl.when(c == nc - 1)
    def _():
        r = _dot(kn_ref[...].astype(BF16), qt2[:HEAD_DIM, :cols]).reshape(keys, n_sub, cols)
        key = lax.broadcasted_iota(jnp.int32, r.shape, 0)
        qry = lax.broadcasted_iota(jnp.int32, r.shape, 2) % ts
        r = jnp.where(key <= qry, r, -BIG)
        _, l_fin, acc_fin = accumulate((m_ref[...], l_ref[...], acc_ref[...]), [r],
                                       [vn_ref[...].astype(BF16)], p_refs[:1])
        lam = _lambda_full(lam_ref, lam_init)
        on = acc_fin / to_col(l_fin)
        for hh in range(n_heads):
            o1 = on[(2 * hh) * ts:(2 * hh + 1) * ts]
            o2 = on[(2 * hh + 1) * ts:(2 * hh + 2) * ts]
            o = o1 - lam * o2
            o_ref[:, hh * V_DIM:(hh + 1) * V_DIM] = _rms(o, sg_ref[...]) * (1.0 - lam_init)


def _decode_call(page_table, q, k_new, v_new, lam_vecs, subln_g, cache_k, cache_v, *, a, lam_init):
    n_seq, n_pages = page_table.shape
    m, d = q.shape
    ts = m // n_seq
    n_heads = d // V_DIM
    n_sub = 2 * n_heads
    cols = n_sub * ts
    npg = PAGES_PER_STEP
    nc = n_pages // npg
    pad = PAGE_SIZE - ts
    qt = jnp.transpose(q.reshape(n_seq, ts, n_sub, HEAD_DIM), (0, 3, 2, 1)).reshape(n_seq, HEAD_DIM, cols)
    z = jnp.zeros_like(qt)
    qt2 = jnp.concatenate([jnp.concatenate([qt, z], axis=2), jnp.concatenate([z, qt], axis=2)], axis=1)
    kn = jnp.pad(k_new.reshape(n_seq, ts, n_sub, HEAD_DIM), ((0, 0), (0, pad), (0, 0), (0, 0)))
    vn = jnp.pad(v_new.reshape(n_seq, ts, n_heads, V_DIM), ((0, 0), (0, pad), (0, 0), (0, 0)))
    kn = kn.reshape(n_seq, PAGE_SIZE * n_sub, HEAD_DIM)
    vn = vn.reshape(n_seq, PAGE_SIZE * n_heads, V_DIM)

    def page_spec(shape, u):
        return pl.BlockSpec((None, None) + shape,
                            lambda b, c, pt: (a, pt[b, c * npg + u], 0, 0, 0))

    k_specs = [page_spec(cache_k.shape[2:], u) for u in range(npg)]
    v_specs = [page_spec(cache_v.shape[2:], u) for u in range(npg)]
    kern = functools.partial(_decode_kernel, n_heads=n_heads, ts=ts, lam_init=lam_init)
    grid_spec = pltpu.PrefetchScalarGridSpec(
        num_scalar_prefetch=1,
        grid=(n_seq, nc),
        in_specs=[
            pl.BlockSpec((None, 2 * HEAD_DIM, 2 * cols), lambda b, c, pt: (b, 0, 0)),
            pl.BlockSpec((None, PAGE_SIZE * n_sub, HEAD_DIM), lambda b, c, pt: (b, 0, 0)),
            pl.BlockSpec((None, PAGE_SIZE * n_heads, V_DIM), lambda b, c, pt: (b, 0, 0)),
            pl.BlockSpec((4, HEAD_DIM), lambda b, c, pt: (0, 0)),
            pl.BlockSpec((1, V_DIM), lambda b, c, pt: (0, 0)),
        ] + k_specs + v_specs,
        out_specs=pl.BlockSpec((ts, d), lambda b, c, pt: (b, 0)),
        scratch_shapes=[
            pltpu.VMEM((n_sub, cols), F32),
            pltpu.VMEM((n_sub, cols), F32),
            pltpu.VMEM((cols, V_DIM), F32),
        ] + [pltpu.VMEM((PAGE_SIZE * n_sub, cols), F32) for _ in range(4)],
    )
    return pl.pallas_call(
        kern,
        grid_spec=grid_spec,
        out_shape=jax.ShapeDtypeStruct((m, d), F32),
        compiler_params=_cparams(("arbitrary", "arbitrary")),
        name="sample_paged_diff_attn",
    )(page_table, qt2, kn, vn, lam_vecs, subln_g.reshape(1, V_DIM),
      *([cache_k] * npg), *([cache_v] * npg))


def _proj_kernel(a_ref, w_ref, x_ref, mod_ref, o_ref, *, per_row, tiles_per_seq):
    seq = pl.program_id(0) // tiles_per_seq
    mix = _dot(a_ref[...].astype(BF16), w_ref[...])
    gmod = _mod_rows(mod_ref, 2, seq, per_row, slice(None))
    o_ref[...] = x_ref[...] + gmod * mix


def _proj_call(act, w, x, mod, *, layer, w_index, tm, tn, per_row, tiles_per_seq):
    m, d = x.shape
    kdim = act.shape[1]
    rows = mod.shape[2]
    kern = functools.partial(_proj_kernel, per_row=per_row, tiles_per_seq=tiles_per_seq)
    return pl.pallas_call(
        kern,
        grid=(m // tm, d // tn),
        in_specs=[
            pl.BlockSpec((tm, kdim), lambda i, n: (i, 0)),
            pl.BlockSpec((None, kdim, tn), lambda i, n: (w_index, 0, n)),
            pl.BlockSpec((tm, tn), lambda i, n: (i, n)),
            pl.BlockSpec((None, 3, rows, tn), lambda i, n: (layer, 1, 0, n)),
        ],
        out_specs=pl.BlockSpec((tm, tn), lambda i, n: (i, n)),
        out_shape=jax.ShapeDtypeStruct((m, d), F32),
        compiler_params=_cparams(("arbitrary", "arbitrary")),
        name="gated_residual_proj",
    )(act, w, x, mod)


def _sgu_in_kernel(x_ref, mod_ref, g_ref, w_ref, u_ref, v_ref, mean_ref, rstd_ref,
                   h_ref, s1_ref, s2_ref, *, per_row, tiles_per_seq, half, row_chunk):
    i = pl.program_id(0)
    n = pl.program_id(1)
    nn = pl.num_programs(1)
    nu = nn // 2
    seq = i // tiles_per_seq
    w = w_ref[...]

    def step(first, is_v):
        for r in range(0, h_ref.shape[0], row_chunk):
            rows = slice(r, r + row_chunk)
            if first:
                h = _modulated(x_ref, mod_ref, g_ref, seq, per_row, rows)
                h_ref[rows, :] = h
            else:
                h = h_ref[rows, :]
            z = _dot(h, w)
            z = 0.5 * z * (1.0 + lax.erf(z * (2.0 ** -0.5)))
            if is_v:
                v_ref[rows, :] = z
                s1_ref[rows, :] += jnp.sum(z, axis=-1, keepdims=True)
                s2_ref[rows, :] += jnp.sum(z * z, axis=-1, keepdims=True)
            else:
                u_ref[rows, :] = z.astype(u_ref.dtype)

    @pl.when(n == 0)
    def _():
        s1_ref[...] = jnp.zeros(s1_ref.shape, F32)
        s2_ref[...] = jnp.zeros(s2_ref.shape, F32)
        step(True, False)

    @pl.when((n > 0) & (n < nu))
    def _():
        step(False, False)

    @pl.when(n >= nu)
    def _():
        step(False, True)

    @pl.when(n == nn - 1)
    def _():
        mean = s1_ref[...] * (1.0 / half)
        var = s2_ref[...] * (1.0 / half) - mean * mean
        mean_ref[...] = mean
        rstd_ref[...] = lax.rsqrt(var + EPS)


def _sgu_in_call(x, mod, norm_g, w_in, *, layer, g, tm, per_row, tiles_per_seq):
    m, d = x.shape
    ffn = w_in.shape[2]
    half = ffn // 2
    tn = 512
    nn = ffn // tn
    nu = nn // 2
    assert nu >= 2
    rows = mod.shape[2]
    g3 = norm_g.reshape(norm_g.shape[0], norm_g.shape[1], 1, d)
    kern = functools.partial(_sgu_in_kernel, per_row=per_row, tiles_per_seq=tiles_per_seq, half=half,
                             row_chunk=min(tm, ROW_CHUNK))
    return pl.pallas_call(
        kern,
        grid=(m // tm, nn),
        in_specs=[
            pl.BlockSpec((tm, d), lambda i, n: (i, 0)),
            pl.BlockSpec((None, 3, rows, d), lambda i, n: (layer, 1, 0, 0)),
            pl.BlockSpec((None, None, 1, d), lambda i, n: (layer, 1, 0, 0)),
            pl.BlockSpec((None, d, tn), lambda i, n: (g, 0, n)),
        ],
        out_specs=[
            pl.BlockSpec((tm, tn), lambda i, n: (i, jnp.minimum(n, nu - 1))),
            pl.BlockSpec((tm, tn), lambda i, n: (i, jnp.maximum(n - nu, 0))),
            pl.BlockSpec((tm, 1), lambda i, n: (i, 0)),
            pl.BlockSpec((tm, 1), lambda i, n: (i, 0)),
        ],
        out_shape=[
            jax.ShapeDtypeStruct((m, half), BF16),
            jax.ShapeDtypeStruct((m, half), F32),
            jax.ShapeDtypeStruct((m, 1), F32),
            jax.ShapeDtypeStruct((m, 1), F32),
        ],
        scratch_shapes=[pltpu.VMEM((tm, d), BF16), pltpu.VMEM((tm, 1), F32), pltpu.VMEM((tm, 1), F32)],
        compiler_params=_cparams(("arbitrary", "arbitrary")),
        name="sgu_in",
    )(x, mod, g3, w_in)


def _sgu_gate_kernel(u_ref, v_ref, mean_ref, rstd_ref, lg_ref, lb_ref, ws_ref, bs_ref,
                     o_ref, vn_ref, *, chunk_rows, n_chunks):
    vn = (v_ref[...] - mean_ref[...]) * rstd_ref[...] * lg_ref[...] + lb_ref[...]
    if vn_ref is not None:
        vn_ref[...] = vn
    vnb = vn.astype(BF16)
    cr = chunk_rows
    r = lax.broadcasted_iota(jnp.int32, (cr, cr), 0)
    c = lax.broadcasted_iota(jnp.int32, (cr, cr), 1)
    ws = jnp.where(c <= r, ws_ref[:cr, :cr], 0.0).astype(BF16)
    bs = bs_ref[:cr, :]
    for ch in range(n_chunks):
        sl = slice(ch * cr, (ch + 1) * cr)
        s = _dot(ws, vnb[sl]) + bs
        o_ref[sl, :] = (u_ref[sl, :].astype(F32) * s).astype(o_ref.dtype)


def _sgu_gate_call(u, v, mean, rstd, ln_g, ln_b, w_s, b_s, *, g, tm, chunk_rows, want_vn):
    m, half = u.shape
    gd = half // SGU_GROUPS
    n_chunks = tm // chunk_rows
    bs3 = b_s.reshape(b_s.shape[0], SGU_GROUPS, CHUNK, 1)
    lg3 = ln_g.reshape(ln_g.shape[0], 1, half)
    lb3 = ln_b.reshape(ln_b.shape[0], 1, half)

    def kern(*refs):
        if want_vn:
            _sgu_gate_kernel(*refs, chunk_rows=chunk_rows, n_chunks=n_chunks)
        else:
            _sgu_gate_kernel(*refs, None, chunk_rows=chunk_rows, n_chunks=n_chunks)

    blk = pl.BlockSpec((tm, gd), lambda i, j: (i, j))
    out_specs = [blk]
    out_shape = [jax.ShapeDtypeStruct((m, half), BF16)]
    if want_vn:
        out_specs.append(blk)
        out_shape.append(jax.ShapeDtypeStruct((m, half), F32))
    return pl.pallas_call(
        kern,
        grid=(m // tm, SGU_GROUPS),
        in_specs=[
            blk,
            blk,
            pl.BlockSpec((tm, 1), lambda i, j: (i, 0)),
            pl.BlockSpec((tm, 1), lambda i, j: (i, 0)),
            pl.BlockSpec((None, 1, gd), lambda i, j: (g, 0, j)),
            pl.BlockSpec((None, 1, gd), lambda i, j: (g, 0, j)),
            pl.BlockSpec((None, None, CHUNK, CHUNK), lambda i, j: (g, j, 0, 0)),
            pl.BlockSpec((None, None, CHUNK, 1), lambda i, j: (g, j, 0, 0)),
        ],
        out_specs=out_specs,
        out_shape=out_shape,
        compiler_params=_cparams(("arbitrary", "arbitrary")),
        name="sgu_gate",
    )(u, v, mean, rstd, lg3, lb3, w_s, bs3)


def _rotary_table(pos):
    half = HEAD_DIM // 2
    inv_freq = ROPE_THETA ** (-jnp.arange(half, dtype=F32) / half)
    ang = pos.astype(F32)[:, None] * inv_freq[None, :]
    cos, sin = jnp.cos(ang), jnp.sin(ang)
    return jnp.concatenate([cos, cos, -sin, sin], axis=-1)


def kernel(x_prompt, x_sample, c_prompt, c_sample, cache_k, cache_v, page_table, norm_g, w_ada, b_ada,
           ffn_w_in, ffn_w_out, attn_w_qkv, attn_w_o, attn_lambda, attn_subln_g, sgu_w_in, sgu_ln_g,
           sgu_ln_b, sgu_w_s, sgu_b_s, sgu_w_out, final_g):
    nb, t, d = x_prompt.shape
    db, ts, _ = x_sample.shape
    depth = w_ada.shape[0]
    n_pages = page_table.shape[1]
    past = n_pages * PAGE_SIZE
    mp, ms = nb * t, db * ts
    tm = min(ROW_TILE, t)
    tps = t // tm

    ffn_w_in, ffn_w_out, attn_w_qkv, attn_w_o, sgu_w_in, sgu_w_out = (
        w.astype(BF16) for w in (ffn_w_in, ffn_w_out, attn_w_qkv, attn_w_o, sgu_w_in, sgu_w_out))

    c_all = jnp.concatenate([c_prompt, c_sample, jnp.zeros((N_SEQ_PAD - nb - db, d), F32)], axis=0)
    mod = _ada_call(c_all, w_ada, b_ada)
    mod_s = jnp.repeat(mod[:, :, nb:nb + db], ts, axis=2)

    cs_p = _rotary_table(jnp.arange(t, dtype=jnp.int32))
    cs_s = jnp.tile(_rotary_table(past + jnp.arange(ts, dtype=jnp.int32)), (db, 1))

    xp = x_prompt.reshape(mp, d)
    xs = x_sample.reshape(ms, d)
    prm = dict(tm=tm, per_row=False, tiles_per_seq=tps)
    srm = dict(tm=ms, per_row=True, tiles_per_seq=1)
    kv_p = kv_s = None
    new_sgu_v = []

    for i in range(depth):
        last = i == depth - 1
        ffn = functools.partial(_ffn_call, norm_g=norm_g, w_in=ffn_w_in, w_out=ffn_w_out,
                                final_g=final_g, layer=i)
        xp = ffn(xp, mod, sub=0, which=0, final_norm=False, **prm)
        xs = ffn(xs, mod_s, sub=0, which=0, final_norm=False, **srm)
        if i % 2 == 0:
            a = i // 2
            lam_init = 0.8 - 0.6 * math.exp(-0.3 * i)
            qp, kp, vp, kpb, vpt = _qkv_call(xp, mod, norm_g, attn_w_qkv, cs_p, kv_p, layer=i, a=a,
                                             prompt=True, **prm)
            kv_p = (kp, vp)
            op = _attn_call(qp, kpb, vpt, attn_lambda[a], attn_subln_g[a], n_seq=nb, seq_len=t,
                            lam_init=lam_init)
            xp = _proj_call(op, attn_w_o, xp, mod, layer=i, w_index=a, tn=512, **prm)
            qs, ks_, vs_ = _qkv_call(xs, mod_s, norm_g, attn_w_qkv, cs_s, kv_s, layer=i, a=a,
                                     prompt=False, **srm)
            kv_s = (ks_, vs_)
            os_ = _decode_call(page_table, qs, ks_[a], vs_[a], attn_lambda[a], attn_subln_g[a],
                               cache_k, cache_v, a=a, lam_init=lam_init)
            xs = _proj_call(os_, attn_w_o, xs, mod_s, layer=i, w_index=a, tn=512, **srm)
        else:
            g = i // 2
            up, vp_, mean_p, rstd_p = _sgu_in_call(xp, mod, norm_g, sgu_w_in, layer=i, g=g, **prm)
            (gp,) = _sgu_gate_call(up, vp_, mean_p, rstd_p, sgu_ln_g, sgu_ln_b, sgu_w_s, sgu_b_s,
                                   g=g, tm=tm, chunk_rows=min(CHUNK, t), want_vn=False)
            xp = _proj_call(gp, sgu_w_out, xp, mod, layer=i, w_index=g, tn=512, **prm)
            us, vs2, mean_s, rstd_s = _sgu_in_call(xs, mod_s, norm_g, sgu_w_in, layer=i, g=g, **srm)
            gs, vn_s = _sgu_gate_call(us, vs2, mean_s, rstd_s, sgu_ln_g, sgu_ln_b, sgu_w_s, sgu_b_s,
                                      g=g, tm=ms, chunk_rows=ts, want_vn=True)
            xs = _proj_call(gs, sgu_w_out, xs, mod_s, layer=i, w_index=g, tn=512, **srm)
            new_sgu_v.append(vn_s)
        xp = ffn(xp, mod, sub=2, which=1, final_norm=last, **prm)
        xs = ffn(xs, mod_s, sub=2, which=1, final_norm=last, **srm)

    n_sub = 2 * (d // V_DIM)
    y_prompt = xp.reshape(nb, t, d)
    y_sample = xs.reshape(db, ts, d)
    k_prompt = kv_p[0].reshape(-1, nb, t, n_sub, HEAD_DIM)
    v_prompt = kv_p[1].reshape(-1, nb, t, n_sub // 2, V_DIM)
    k_sample = kv_s[0].reshape(-1, db, ts, n_sub, HEAD_DIM)
    v_sample = kv_s[1].reshape(-1, db, ts, n_sub // 2, V_DIM)
    sgu_v_sample = jnp.stack(new_sgu_v).reshape(len(new_sgu_v), db, ts, -1)
    return (y_prompt, y_sample, k_prompt, v_prompt, k_sample, v_sample, sgu_v_sample)
```

```python
import functools
import math

import jax
import jax.numpy as jnp
from jax import lax
from jax.experimental import pallas as pl
from jax.experimental.pallas import tpu as pltpu

F32 = jnp.float32
BF16 = jnp.bfloat16

EPS = 1e-6
ROPE_THETA = 10000.0
HEAD_DIM = 128
V_DIM = 2 * HEAD_DIM
Q_SCALE = HEAD_DIM ** -0.5 * math.log2(math.e)
CHUNK = 128
SGU_GROUPS = 8
PAGE_SIZE = 128
N_MOD = 9
N_SEQ_PAD = 16
BIG = 1e30

VMEM_LIMIT_BYTES = 60 * 1024 * 1024
ROW_TILE = 1024
ROW_CHUNK = 256
ATT_BLOCK = 256
ATT_QUERY_BLOCK = 512
PAGES_PER_STEP = 8


def _cparams(sem):
    return pltpu.CompilerParams(dimension_semantics=sem, vmem_limit_bytes=VMEM_LIMIT_BYTES)


def _dot(a, b):
    return jnp.dot(a, b, preferred_element_type=F32)


def _dot_nt(a, b):
    return lax.dot_general(a, b, (((1,), (1,)), ((), ())), preferred_element_type=F32)


def _dot_tn(a, b):
    return lax.dot_general(a, b, (((0,), (0,)), ((), ())), preferred_element_type=F32)


def _rms(x, g):
    return x * lax.rsqrt(jnp.mean(x * x, axis=-1, keepdims=True) + EPS) * g


def _mod_rows(mod_ref, k, seq, per_row, rows):
    if per_row:
        return mod_ref[k, rows, :]
    return mod_ref[k, pl.ds(seq, 1), :]


def _modulated(x_ref, mod_ref, g_ref, seq, per_row, rows):
    shift = _mod_rows(mod_ref, 0, seq, per_row, rows)
    scale = _mod_rows(mod_ref, 1, seq, per_row, rows)
    return (_rms(x_ref[rows, :], g_ref[...]) * (1.0 + scale) + shift).astype(BF16)


def _ada_kernel(c_ref, w_ref, b_ref, o_ref):
    c = c_ref[...]
    a = (c * jax.nn.sigmoid(c)).astype(BF16)
    o_ref[...] = _dot(a, w_ref[...].astype(BF16)) + b_ref[...]


def _ada_call(c_all, w_ada, b_ada):
    depth, d, _ = w_ada.shape
    tn = 1024
    nt = d // tn
    b4 = b_ada.reshape(depth, N_MOD, 1, d)
    return pl.pallas_call(
        _ada_kernel,
        grid=(depth, N_MOD * nt),
        in_specs=[
            pl.BlockSpec((N_SEQ_PAD, d), lambda l, n: (0, 0)),
            pl.BlockSpec((None, d, tn), lambda l, n: (l, 0, n)),
            pl.BlockSpec((None, None, 1, tn), lambda l, n: (l, n // nt, 0, n % nt)),
        ],
        out_specs=pl.BlockSpec((None, None, N_SEQ_PAD, tn), lambda l, n: (l, n // nt, 0, n % nt)),
        out_shape=jax.ShapeDtypeStruct((depth, N_MOD, N_SEQ_PAD, d), F32),
        compiler_params=_cparams(("arbitrary", "arbitrary")),
        name="adaln",
    )(c_all, w_ada, b4)


def _ffn_kernel(x_ref, mod_ref, g_ref, wg_ref, wu_ref, wo_ref, fg_ref, o_ref, h_ref, *,
                per_row, tiles_per_seq, final_norm, row_chunk):
    i = pl.program_id(0)
    j = pl.program_id(1)
    nj = pl.num_programs(1)
    seq = i // tiles_per_seq
    wg = wg_ref[...]
    wu = wu_ref[...]
    wo = wo_ref[...]

    def step(first, last):
        for r in range(0, h_ref.shape[0], row_chunk):
            rows = slice(r, r + row_chunk)
            if first:
                h = _modulated(x_ref, mod_ref, g_ref, seq, per_row, rows)
                h_ref[rows, :] = h
            else:
                h = h_ref[rows, :]
            gate = _dot(h, wg)
            up = _dot(h, wu)
            a = (gate * jax.nn.sigmoid(gate) * up).astype(BF16)
            tot = _dot(a, wo)
            if not first:
                tot = o_ref[rows, :] + tot
            if last:
                gmod = _mod_rows(mod_ref, 2, seq, per_row, rows)
                tot = x_ref[rows, :] + 0.5 * gmod * tot
                if final_norm:
                    tot = _rms(tot, fg_ref[...])
            o_ref[rows, :] = tot

    @pl.when(j == 0)
    def _():
        step(True, False)

    @pl.when((j > 0) & (j < nj - 1))
    def _():
        step(False, False)

    @pl.when(j == nj - 1)
    def _():
        step(False, True)


def _ffn_call(x, mod, norm_g, w_in, w_out, final_g, *, layer, sub, which, tm, per_row,
              tiles_per_seq, final_norm, row_chunk=ROW_CHUNK):
    m, d = x.shape
    d_ff = w_out.shape[2]
    tf = 512
    nj = d_ff // tf
    assert nj >= 2
    rows = mod.shape[2]
    g3 = norm_g.reshape(norm_g.shape[0], norm_g.shape[1], 1, d)
    kern = functools.partial(_ffn_kernel, per_row=per_row, tiles_per_seq=tiles_per_seq,
                             final_norm=final_norm, row_chunk=min(tm, row_chunk))
    return pl.pallas_call(
        kern,
        grid=(m // tm, nj),
        in_specs=[
            pl.BlockSpec((tm, d), lambda i, j: (i, 0), pipeline_mode=pl.Buffered(1)),
            pl.BlockSpec((None, 3, rows, d), lambda i, j: (layer, sub, 0, 0)),
            pl.BlockSpec((None, None, 1, d), lambda i, j: (layer, sub, 0, 0)),
            pl.BlockSpec((None, None, d, tf), lambda i, j: (layer, which, 0, j)),
            pl.BlockSpec((None, None, d, tf), lambda i, j: (layer, which, 0, nj + j)),
            pl.BlockSpec((None, None, tf, d), lambda i, j: (layer, which, j, 0)),
            pl.BlockSpec((1, d), lambda i, j: (0, 0)),
        ],
        out_specs=pl.BlockSpec((tm, d), lambda i, j: (i, 0)),
        out_shape=jax.ShapeDtypeStruct((m, d), F32),
        scratch_shapes=[pltpu.VMEM((tm, d), BF16)],
        compiler_params=_cparams(("arbitrary", "arbitrary")),
        name="ffn_half_step",
    )(x, mod, g3, w_in, w_in, w_out, final_g.reshape(1, d))


def _qkv_kernel(x_ref, mod_ref, g_ref, w_ref, cs_ref, *rest, per_row, tiles_per_seq, tn, prompt,
                aliased, row_chunk):
    if aliased:
        rest = rest[2:]
    if prompt:
        q_ref, k_ref, v_ref, kb_ref, vt_ref, h_ref = rest
    else:
        q_ref, k_ref, v_ref, h_ref = rest
    i = pl.program_id(0)
    p = pl.program_id(1)
    n = pl.program_id(2)
    seq = i // tiles_per_seq
    w = w_ref[...]

    def step(part, first):
        for ci, r in enumerate(range(0, h_ref.shape[0], row_chunk)):
            rows = slice(r, r + row_chunk)
            if first:
                h = _modulated(x_ref, mod_ref, g_ref, seq, per_row, rows)
                h_ref[rows, :] = h
            else:
                h = h_ref[rows, :]
            acc = _dot(h, w)
            if part == 2:
                v_ref[rows, :] = acc
                if prompt:
                    vt_ref[ci] = acc.T.astype(BF16)
                continue
            cos2 = cs_ref[rows, :HEAD_DIM]
            sin2 = cs_ref[rows, HEAD_DIM:]
            for hh in range(tn // HEAD_DIM):
                sl = slice(hh * HEAD_DIM, (hh + 1) * HEAD_DIM)
                xh = acc[:, sl]
                rot = xh * cos2 + pltpu.roll(xh, HEAD_DIM // 2, 1) * sin2
                if part == 0:
                    q_ref[rows, sl] = (rot * Q_SCALE).astype(q_ref.dtype)
                else:
                    k_ref[rows, sl] = rot
                    if prompt:
                        kb_ref[rows, sl] = rot.astype(BF16)

    @pl.when((p == 0) & (n == 0))
    def _():
        step(0, True)

    @pl.when((p == 0) & (n > 0))
    def _():
        step(0, False)

    @pl.when(p == 1)
    def _():
        step(1, False)

    @pl.when(p == 2)
    def _():
        step(2, False)


def _qkv_call(x, mod, norm_g, w_qkv, cs, kv_prev, *, layer, a, tm, per_row, tiles_per_seq, prompt):
    m, d = x.shape
    n_attn = w_qkv.shape[0]
    tn = 512
    nt = d // tn
    rows = mod.shape[2]
    g3 = norm_g.reshape(norm_g.shape[0], norm_g.shape[1], 1, d)
    n_cs_tiles = cs.shape[0] // tm
    row_chunk = min(tm, ATT_BLOCK)
    aliased = kv_prev is not None
    kern = functools.partial(_qkv_kernel, per_row=per_row, tiles_per_seq=tiles_per_seq, tn=tn,
                             prompt=prompt, aliased=aliased, row_chunk=row_chunk)

    def part_col(part, p, n):
        return jnp.where(p < part, 0, jnp.where(p == part, n, nt - 1))

    def out_block(part):
        return pl.BlockSpec((tm, tn), lambda i, p, n: (i, part_col(part, p, n)))

    def stacked_block(part):
        return pl.BlockSpec((None, tm, tn), lambda i, p, n: (a, i, part_col(part, p, n)))

    in_specs = [
        pl.BlockSpec((tm, d), lambda i, p, n: (i, 0)),
        pl.BlockSpec((None, 3, rows, d), lambda i, p, n: (layer, 1, 0, 0)),
        pl.BlockSpec((None, None, 1, d), lambda i, p, n: (layer, 1, 0, 0)),
        pl.BlockSpec((None, d, tn), lambda i, p, n: (a, 0, p * nt + n)),
        pl.BlockSpec((tm, 2 * HEAD_DIM), lambda i, p, n: (i % n_cs_tiles, 0)),
    ]
    args = [x, mod, g3, w_qkv, cs]
    aliases = {}
    if aliased:
        in_specs += [pl.BlockSpec(memory_space=pl.ANY), pl.BlockSpec(memory_space=pl.ANY)]
        args += list(kv_prev)
        aliases = {5: 1, 6: 2}
    out_specs = [out_block(0), stacked_block(1), stacked_block(2)]
    out_shape = [
        jax.ShapeDtypeStruct((m, d), BF16 if prompt else F32),
        jax.ShapeDtypeStruct((n_attn, m, d), F32),
        jax.ShapeDtypeStruct((n_attn, m, d), F32),
    ]
    if prompt:
        nb = tm // ATT_BLOCK
        out_specs += [out_block(1),
                      pl.BlockSpec((nb, tn, ATT_BLOCK), lambda i, p, n: (i, part_col(2, p, n), 0))]
        out_shape += [jax.ShapeDtypeStruct((m, d), BF16),
                      jax.ShapeDtypeStruct((m // ATT_BLOCK, d, ATT_BLOCK), BF16)]
    return pl.pallas_call(
        kern,
        grid=(m // tm, 3, nt),
        in_specs=in_specs,
        out_specs=out_specs,
        out_shape=out_shape,
        input_output_aliases=aliases,
        scratch_shapes=[pltpu.VMEM((tm, d), BF16)],
        compiler_params=_cparams(("arbitrary", "arbitrary", "arbitrary")),
        name="attn_qkv",
    )(*args)


def _lambda_full(lam_ref, lam_init):
    lv = lam_ref[...]
    d1 = jnp.sum(lv[0:1] * lv[1:2], axis=-1, keepdims=True)
    d2 = jnp.sum(lv[2:3] * lv[3:4], axis=-1, keepdims=True)
    return jnp.exp(d1) - jnp.exp(d2) + lam_init


def _attn_kernel(q_ref, k_ref, vt_ref, lam_ref, sg_ref, o_ref, acc_ref, m_ref, l_ref, *,
                 tq, blk, lam_init):
    qi = pl.program_id(2)
    kq = tq // blk
    q = q_ref[...]
    q1 = q[:, :HEAD_DIM]
    q2 = q[:, HEAD_DIM:]
    acc_ref[...] = jnp.zeros(acc_ref.shape, F32)
    m_ref[...] = jnp.full(m_ref.shape, -BIG, F32)
    l_ref[...] = jnp.zeros(l_ref.shape, F32)

    def scores(j):
        start = pl.multiple_of(j * tq, tq)
        kb = k_ref[pl.ds(start, tq), :]
        return jnp.concatenate([_dot_nt(kb[:, :HEAD_DIM], q1), _dot_nt(kb[:, HEAD_DIM:], q2)], axis=1)

    def update(s, j, diagonal):
        vt = jnp.concatenate([vt_ref[j * kq + d] for d in range(kq)], axis=1)
        if diagonal:
            key = lax.broadcasted_iota(jnp.int32, s.shape, 0)
            qry = lax.broadcasted_iota(jnp.int32, s.shape, 1) % tq
            s = jnp.where(key <= qry, s, -BIG)
        m_old = m_ref[...]
        m_new = jnp.maximum(m_old, jnp.max(s, axis=0, keepdims=True))
        p = jnp.exp2(s - m_new)
        alpha = jnp.exp2(m_old - m_new)
        m_ref[...] = m_new
        l_ref[...] = alpha * l_ref[...] + jnp.sum(p, axis=0, keepdims=True)
        acc_ref[...] = alpha * acc_ref[...] + _dot(vt, p.astype(BF16))

    def earlier_keys(j, s):
        s_next = scores(j + 1)
        update(s, j, False)
        return s_next

    s_diag = lax.fori_loop(0, qi, earlier_keys, scores(0))
    update(s_diag, qi, True)

    lam = _lambda_full(lam_ref, lam_init)
    on = acc_ref[...] / l_ref[...]
    o = on[:, :tq] - lam * on[:, tq:]
    o = o * lax.rsqrt(jnp.mean(o * o, axis=0, keepdims=True) + EPS) * sg_ref[...]
    o_ref[...] = (o * (1.0 - lam_init)).T.astype(o_ref.dtype)


def _attn_call(q, kb, vt, lam_vecs, subln_g, *, n_seq, seq_len, lam_init):
    m, d = q.shape
    n_heads = d // V_DIM
    blk = ATT_BLOCK
    tq = min(ATT_QUERY_BLOCK, seq_len)
    nq = seq_len // tq
    kern = functools.partial(_attn_kernel, tq=tq, blk=blk, lam_init=lam_init)
    return pl.pallas_call(
        kern,
        grid=(n_seq, n_heads, nq),
        in_specs=[
            pl.BlockSpec((tq, V_DIM), lambda b, h, qi: (b * nq + qi, h)),
            pl.BlockSpec((seq_len, V_DIM), lambda b, h, qi: (b, h)),
            pl.BlockSpec((seq_len // blk, V_DIM, blk), lambda b, h, qi: (b, h, 0)),
            pl.BlockSpec((4, HEAD_DIM), lambda b, h, qi: (0, 0)),
            pl.BlockSpec((V_DIM, 1), lambda b, h, qi: (0, 0)),
        ],
        out_specs=pl.BlockSpec((tq, V_DIM), lambda b, h, qi: (b * nq + qi, h)),
        out_shape=jax.ShapeDtypeStruct((m, d), BF16),
        scratch_shapes=[pltpu.VMEM((V_DIM, 2 * tq), F32), pltpu.VMEM((1, 2 * tq), F32),
                        pltpu.VMEM((1, 2 * tq), F32)],
        compiler_params=_cparams(("arbitrary", "arbitrary", "arbitrary")),
        name="prompt_diff_attn",
    )(q, kb, vt, lam_vecs, subln_g.reshape(V_DIM, 1))


def _decode_kernel(pt_ref, qt_ref, kn_ref, vn_ref, lam_ref, sg_ref, *rest, n_heads, ts, lam_init):
    del pt_ref
    npg = PAGES_PER_STEP
    k_refs = rest[:npg]
    v_refs = rest[npg:2 * npg]
    o_ref, m_ref, l_ref, acc_ref = rest[2 * npg:2 * npg + 4]
    p_refs = rest[2 * npg + 4:]
    c = pl.program_id(1)
    nc = pl.num_programs(1)
    n_sub = 2 * n_heads
    cols = n_sub * ts
    keys = PAGE_SIZE

    hrow = lax.broadcasted_iota(jnp.int32, (n_sub, cols), 0)
    hlane = lax.broadcasted_iota(jnp.int32, (n_sub, cols), 1) // ts
    valid = hrow == hlane

    def to_col(a):
        row = jnp.sum(jnp.where(valid, a, 0.0), axis=0, keepdims=True)
        ri = lax.broadcasted_iota(jnp.int32, (cols, cols), 0)
        ci = lax.broadcasted_iota(jnp.int32, (cols, cols), 1)
        return jnp.sum(jnp.where(ri == ci, jnp.broadcast_to(row, (cols, cols)), 0.0), axis=1, keepdims=True)

    @pl.when(c == 0)
    def _():
        m_ref[...] = jnp.where(valid, -BIG, BIG)
        l_ref[...] = jnp.zeros(l_ref.shape, F32)
        acc_ref[...] = jnp.zeros(acc_ref.shape, F32)

    qt2 = qt_ref[...].astype(BF16)

    def flat_k(ref):
        return ref[...].reshape(keys * n_sub, HEAD_DIM).astype(BF16)

    def flat_v(ref):
        return ref[...].reshape(keys * n_heads, V_DIM).astype(BF16)

    def accumulate(state, r_list, v_list, scratch):
        m_old, l_old, acc = state
        m_new = m_old
        for r in r_list:
            m_new = jnp.maximum(m_new, jnp.max(r, axis=0))
        alpha = jnp.exp2(m_old - m_new)
        l_new = alpha * l_old
        o = None
        for r, v, p_ref in zip(r_list, v_list, scratch):
            p = jnp.exp2(r - m_new[None])
            l_new = l_new + jnp.sum(p, axis=0)
            p_ref[...] = p.reshape(keys * n_sub, cols)
            ps = (p_ref[pl.ds(0, keys * n_heads, stride=2), :]
                  + p_ref[pl.ds(1, keys * n_heads, stride=2), :]).astype(BF16)
            part = _dot_tn(ps, v)
            o = part if o is None else o + part
        return m_new, l_new, to_col(alpha) * acc + o

    def pair_scores(u):
        k2 = jnp.concatenate([flat_k(k_refs[u]), flat_k(k_refs[u + 1])], axis=1)
        r2 = _dot(k2, qt2)
        return [r2[:, :cols].reshape(keys, n_sub, cols), r2[:, cols:].reshape(keys, n_sub, cols)]

    state = (m_ref[...], l_ref[...], acc_ref[...])
    r_next = pair_scores(0)
    for u in range(0, npg, 2):
        r_pair = r_next
        if u + 2 < npg:
            r_next = pair_scores(u + 2)
        slot = u % 4
        state = accumulate(state, r_pair, [flat_v(v_refs[u]), flat_v(v_refs[u + 1])],
                           p_refs[slot:slot + 2])
    m_ref[...], l_ref[...], acc_ref[...] = state

    @pl.when(c == nc - 1)
    def _():
        r = _dot(kn_ref[...].astype(BF16), qt2[:HEAD_DIM, :cols]).reshape(keys, n_sub, cols)
        key = lax.broadcasted_iota(jnp.int32, r.shape, 0)
        qry = lax.broadcasted_iota(jnp.int32, r.shape, 2) % ts
        r = jnp.where(key <= qry, r, -BIG)
        _, l_fin, acc_fin = accumulate((m_ref[...], l_ref[...], acc_ref[...]), [r],
                                       [vn_ref[...].astype(BF16)], p_refs[:1])
        lam = _lambda_full(lam_ref, lam_init)
        on = acc_fin / to_col(l_fin)
        for hh in range(n_heads):
            o1 = on[(2 * hh) * ts:(2 * hh + 1) * ts]
            o2 = on[(2 * hh + 1) * ts:(2 * hh + 2) * ts]
            o = o1 - lam * o2
            o_ref[:, hh * V_DIM:(hh + 1) * V_DIM] = _rms(o, sg_ref[...]) * (1.0 - lam_init)


def _decode_call(page_table, q, k_new, v_new, lam_vecs, subln_g, cache_k, cache_v, *, a, lam_init):
    n_seq, n_pages = page_table.shape
    m, d = q.shape
    ts = m // n_seq
    n_heads = d // V_DIM
    n_sub = 2 * n_heads
    cols = n_sub * ts
    npg = PAGES_PER_STEP
    nc = n_pages // npg
    pad = PAGE_SIZE - ts
    qt = jnp.transpose(q.reshape(n_seq, ts, n_sub, HEAD_DIM), (0, 3, 2, 1)).reshape(n_seq, HEAD_DIM, cols)
    z = jnp.zeros_like(qt)
    qt2 = jnp.concatenate([jnp.concatenate([qt, z], axis=2), jnp.concatenate([z, qt], axis=2)], axis=1)
    kn = jnp.pad(k_new.reshape(n_seq, ts, n_sub, HEAD_DIM), ((0, 0), (0, pad), (0, 0), (0, 0)))
    vn = jnp.pad(v_new.reshape(n_seq, ts, n_heads, V_DIM), ((0, 0), (0, pad), (0, 0), (0, 0)))
    kn = kn.reshape(n_seq, PAGE_SIZE * n_sub, HEAD_DIM)
    vn = vn.reshape(n_seq, PAGE_SIZE * n_heads, V_DIM)

    def page_spec(shape, u):
        return pl.BlockSpec((None, None) + shape,
                            lambda b, c, pt: (a, pt[b, c * npg + u], 0, 0, 0))

    k_specs = [page_spec(cache_k.shape[2:], u) for u in range(npg)]
    v_specs = [page_spec(cache_v.shape[2:], u) for u in range(npg)]
    kern = functools.partial(_decode_kernel, n_heads=n_heads, ts=ts, lam_init=lam_init)
    grid_spec = pltpu.PrefetchScalarGridSpec(
        num_scalar_prefetch=1,
        grid=(n_seq, nc),
        in_specs=[
            pl.BlockSpec((None, 2 * HEAD_DIM, 2 * cols), lambda b, c, pt: (b, 0, 0)),
            pl.BlockSpec((None, PAGE_SIZE * n_sub, HEAD_DIM), lambda b, c, pt: (b, 0, 0)),
            pl.BlockSpec((None, PAGE_SIZE * n_heads, V_DIM), lambda b, c, pt: (b, 0, 0)),
            pl.BlockSpec((4, HEAD_DIM), lambda b, c, pt: (0, 0)),
            pl.BlockSpec((1, V_DIM), lambda b, c, pt: (0, 0)),
        ] + k_specs + v_specs,
        out_specs=pl.BlockSpec((ts, d), lambda b, c, pt: (b, 0)),
        scratch_shapes=[
            pltpu.VMEM((n_sub, cols), F32),
            pltpu.VMEM((n_sub, cols), F32),
            pltpu.VMEM((cols, V_DIM), F32),
        ] + [pltpu.VMEM((PAGE_SIZE * n_sub, cols), F32) for _ in range(4)],
    )
    return pl.pallas_call(
        kern,
        grid_spec=grid_spec,
        out_shape=jax.ShapeDtypeStruct((m, d), F32),
        compiler_params=_cparams(("arbitrary", "arbitrary")),
        name="sample_paged_diff_attn",
    )(page_table, qt2, kn, vn, lam_vecs, subln_g.reshape(1, V_DIM),
      *([cache_k] * npg), *([cache_v] * npg))


def _proj_kernel(a_ref, w_ref, x_ref, mod_ref, o_ref, *, per_row, tiles_per_seq):
    seq = pl.program_id(0) // tiles_per_seq
    mix = _dot(a_ref[...].astype(BF16), w_ref[...])
    gmod = _mod_rows(mod_ref, 2, seq, per_row, slice(None))
    o_ref[...] = x_ref[...] + gmod * mix


def _proj_call(act, w, x, mod, *, layer, w_index, tm, tn, per_row, tiles_per_seq):
    m, d = x.shape
    kdim = act.shape[1]
    rows = mod.shape[2]
    kern = functools.partial(_proj_kernel, per_row=per_row, tiles_per_seq=tiles_per_seq)
    return pl.pallas_call(
        kern,
        grid=(m // tm, d // tn),
        in_specs=[
            pl.BlockSpec((tm, kdim), lambda i, n: (i, 0)),
            pl.BlockSpec((None, kdim, tn), lambda i, n: (w_index, 0, n)),
            pl.BlockSpec((tm, tn), lambda i, n: (i, n)),
            pl.BlockSpec((None, 3, rows, tn), lambda i, n: (layer, 1, 0, n)),
        ],
        out_specs=pl.BlockSpec((tm, tn), lambda i, n: (i, n)),
        out_shape=jax.ShapeDtypeStruct((m, d), F32),
        compiler_params=_cparams(("arbitrary", "arbitrary")),
        name="gated_residual_proj",
    )(act, w, x, mod)


def _sgu_in_kernel(x_ref, mod_ref, g_ref, w_ref, u_ref, v_ref, mean_ref, rstd_ref,
                   h_ref, s1_ref, s2_ref, *, per_row, tiles_per_seq, half, row_chunk):
    i = pl.program_id(0)
    n = pl.program_id(1)
    nn = pl.num_programs(1)
    nu = nn // 2
    seq = i // tiles_per_seq
    w = w_ref[...]

    def step(first, is_v):
        for r in range(0, h_ref.shape[0], row_chunk):
            rows = slice(r, r + row_chunk)
            if first:
                h = _modulated(x_ref, mod_ref, g_ref, seq, per_row, rows)
                h_ref[rows, :] = h
            else:
                h = h_ref[rows, :]
            z = _dot(h, w)
            z = 0.5 * z * (1.0 + lax.erf(z * (2.0 ** -0.5)))
            if is_v:
                v_ref[rows, :] = z
                s1_ref[rows, :] += jnp.sum(z, axis=-1, keepdims=True)
                s2_ref[rows, :] += jnp.sum(z * z, axis=-1, keepdims=True)
            else:
                u_ref[rows, :] = z.astype(u_ref.dtype)

    @pl.when(n == 0)
    def _():
        s1_ref[...] = jnp.zeros(s1_ref.shape, F32)
        s2_ref[...] = jnp.zeros(s2_ref.shape, F32)
        step(True, False)

    @pl.when((n > 0) & (n < nu))
    def _():
        step(False, False)

    @pl.when(n >= nu)
    def _():
        step(False, True)

    @pl.when(n == nn - 1)
    def _():
        mean = s1_ref[...] * (1.0 / half)
        var = s2_ref[...] * (1.0 / half) - mean * mean
        mean_ref[...] = mean
        rstd_ref[...] = lax.rsqrt(var + EPS)


def _sgu_in_call(x, mod, norm_g, w_in, *, layer, g, tm, per_row, tiles_per_seq, row_chunk=ROW_CHUNK):
    m, d = x.shape
    ffn = w_in.shape[2]
    half = ffn // 2
    tn = 512
    nn = ffn // tn
    nu = nn // 2
    assert nu >= 2
    rows = mod.shape[2]
    g3 = norm_g.reshape(norm_g.shape[0], norm_g.shape[1], 1, d)
    kern = functools.partial(_sgu_in_kernel, per_row=per_row, tiles_per_seq=tiles_per_seq, half=half,
                             row_chunk=min(tm, row_chunk))
    return pl.pallas_call(
        kern,
        grid=(m // tm, nn),
        in_specs=[
            pl.BlockSpec((tm, d), lambda i, n: (i, 0)),
            pl.BlockSpec((None, 3, rows, d), lambda i, n: (layer, 1, 0, 0)),
            pl.BlockSpec((None, None, 1, d), lambda i, n: (layer, 1, 0, 0)),
            pl.BlockSpec((None, d, tn), lambda i, n: (g, 0, n)),
        ],
        out_specs=[
            pl.BlockSpec((tm, tn), lambda i, n: (i, jnp.minimum(n, nu - 1))),
            pl.BlockSpec((tm, tn), lambda i, n: (i, jnp.maximum(n - nu, 0))),
            pl.BlockSpec((tm, 1), lambda i, n: (i, 0)),
            pl.BlockSpec((tm, 1), lambda i, n: (i, 0)),
        ],
        out_shape=[
            jax.ShapeDtypeStruct((m, half), BF16),
            jax.ShapeDtypeStruct((m, half), F32),
            jax.ShapeDtypeStruct((m, 1), F32),
            jax.ShapeDtypeStruct((m, 1), F32),
        ],
        scratch_shapes=[pltpu.VMEM((tm, d), BF16), pltpu.VMEM((tm, 1), F32), pltpu.VMEM((tm, 1), F32)],
        compiler_params=_cparams(("arbitrary", "arbitrary")),
        name="sgu_in",
    )(x, mod, g3, w_in)


def _sgu_out_kernel(u_ref, v_ref, mean_ref, rstd_ref, lg_ref, lb_ref, ws_ref, bs_ref, w_ref, x_ref,
                    mod_ref, o_ref, vn_ref, *, per_row, tiles_per_seq, chunk_rows, row_chunk):
    i = pl.program_id(0)
    j = pl.program_id(1)
    nj = pl.num_programs(1)
    seq = i // tiles_per_seq
    cr = chunk_rows
    r = lax.broadcasted_iota(jnp.int32, (cr, cr), 0)
    c = lax.broadcasted_iota(jnp.int32, (cr, cr), 1)
    ws = jnp.where(c <= r, ws_ref[:cr, :cr], 0.0).astype(BF16)
    bs = bs_ref[:cr, :]
    w = w_ref[...]

    def step(first, last):
        for r0 in range(0, u_ref.shape[0], row_chunk):
            rows = slice(r0, r0 + row_chunk)
            gated = []
            for c0 in range(r0, r0 + row_chunk, cr):
                sl = slice(c0, c0 + cr)
                vn = (v_ref[sl, :] - mean_ref[sl, :]) * rstd_ref[sl, :] * lg_ref[...] + lb_ref[...]
                if vn_ref is not None:
                    vn_ref[sl, :] = vn
                s = _dot(ws, vn.astype(BF16)) + bs
                gated.append(u_ref[sl, :].astype(F32) * s)
            tot = _dot(jnp.concatenate(gated, axis=0).astype(BF16), w)
            if not first:
                tot = o_ref[rows, :] + tot
            if last:
                tot = x_ref[rows, :] + _mod_rows(mod_ref, 2, seq, per_row, rows) * tot
            o_ref[rows, :] = tot

    @pl.when(j == 0)
    def _():
        step(True, False)

    @pl.when((j > 0) & (j < nj - 1))
    def _():
        step(False, False)

    @pl.when(j == nj - 1)
    def _():
        step(False, True)


def _sgu_out_call(u, v, mean, rstd, ln_g, ln_b, w_s, b_s, w_out, x, mod, *, layer, g, tm, per_row,
                  tiles_per_seq, chunk_rows, want_vn, row_chunk=ROW_CHUNK):
    m, half = u.shape
    d = x.shape[1]
    gd = half // SGU_GROUPS
    rows = mod.shape[2]
    bs3 = b_s.reshape(b_s.shape[0], SGU_GROUPS, CHUNK, 1)
    lg3 = ln_g.reshape(ln_g.shape[0], 1, half)
    lb3 = ln_b.reshape(ln_b.shape[0], 1, half)
    params = dict(per_row=per_row, tiles_per_seq=tiles_per_seq, chunk_rows=chunk_rows,
                  row_chunk=min(tm, row_chunk))

    def kern(*refs):
        if want_vn:
            _sgu_out_kernel(*refs, **params)
        else:
            _sgu_out_kernel(*refs, None, **params)

    blk = pl.BlockSpec((tm, gd), lambda i, j: (i, j))
    out_specs = [pl.BlockSpec((tm, d), lambda i, j: (i, 0))]
    out_shape = [jax.ShapeDtypeStruct((m, d), F32)]
    if want_vn:
        out_specs.append(blk)
        out_shape.append(jax.ShapeDtypeStruct((m, half), F32))
    return pl.pallas_call(
        kern,
        grid=(m // tm, SGU_GROUPS),
        in_specs=[
            blk,
            blk,
            pl.BlockSpec((tm, 1), lambda i, j: (i, 0)),
            pl.BlockSpec((tm, 1), lambda i, j: (i, 0)),
            pl.BlockSpec((None, 1, gd), lambda i, j: (g, 0, j)),
            pl.BlockSpec((None, 1, gd), lambda i, j: (g, 0, j)),
            pl.BlockSpec((None, None, CHUNK, CHUNK), lambda i, j: (g, j, 0, 0)),
            pl.BlockSpec((None, None, CHUNK, 1), lambda i, j: (g, j, 0, 0)),
            pl.BlockSpec((None, gd, d), lambda i, j: (g, j, 0)),
            pl.BlockSpec((tm, d), lambda i, j: (i, 0), pipeline_mode=pl.Buffered(1)),
            pl.BlockSpec((None, 3, rows, d), lambda i, j: (layer, 1, 0, 0)),
        ],
        out_specs=out_specs,
        out_shape=out_shape,
        compiler_params=_cparams(("arbitrary", "arbitrary")),
        name="sgu_gate_out",
    )(u, v, mean, rstd, lg3, lb3, w_s, bs3, w_out, x, mod)


def _rotary_table(pos):
    half = HEAD_DIM // 2
    inv_freq = ROPE_THETA ** (-jnp.arange(half, dtype=F32) / half)
    ang = pos.astype(F32)[:, None] * inv_freq[None, :]
    cos, sin = jnp.cos(ang), jnp.sin(ang)
    return jnp.concatenate([cos, cos, -sin, sin], axis=-1)


def kernel(x_prompt, x_sample, c_prompt, c_sample, cache_k, cache_v, page_table, norm_g, w_ada, b_ada,
           ffn_w_in, ffn_w_out, attn_w_qkv, attn_w_o, attn_lambda, attn_subln_g, sgu_w_in, sgu_ln_g,
           sgu_ln_b, sgu_w_s, sgu_b_s, sgu_w_out, final_g):
    nb, t, d = x_prompt.shape
    db, ts, _ = x_sample.shape
    depth = w_ada.shape[0]
    n_pages = page_table.shape[1]
    past = n_pages * PAGE_SIZE
    mp, ms = nb * t, db * ts
    tm = min(ROW_TILE, t)
    tps = t // tm

    ffn_w_in, ffn_w_out, attn_w_qkv, attn_w_o, sgu_w_in, sgu_w_out = (
        w.astype(BF16) for w in (ffn_w_in, ffn_w_out, attn_w_qkv, attn_w_o, sgu_w_in, sgu_w_out))

    c_all = jnp.concatenate([c_prompt, c_sample, jnp.zeros((N_SEQ_PAD - nb - db, d), F32)], axis=0)
    mod = _ada_call(c_all, w_ada, b_ada)
    mod_s = jnp.repeat(mod[:, :, nb:nb + db], ts, axis=2)

    cs_p = _rotary_table(jnp.arange(t, dtype=jnp.int32))
    cs_s = jnp.tile(_rotary_table(past + jnp.arange(ts, dtype=jnp.int32)), (db, 1))

    xp = x_prompt.reshape(mp, d)
    xs = x_sample.reshape(ms, d)
    prm = dict(tm=tm, per_row=False, tiles_per_seq=tps)
    srm = dict(tm=ms, per_row=True, tiles_per_seq=1)
    kv_p = kv_s = None
    new_sgu_v = []

    for i in range(depth):
        last = i == depth - 1
        ffn = functools.partial(_ffn_call, norm_g=norm_g, w_in=ffn_w_in, w_out=ffn_w_out,
                                final_g=final_g, layer=i, row_chunk=(256, 512, 1024, 512)[i % 4])
        xp = ffn(xp, mod, sub=0, which=0, final_norm=False, **prm)
        xs = ffn(xs, mod_s, sub=0, which=0, final_norm=False, **srm)
        if i % 2 == 0:
            a = i // 2
            lam_init = 0.8 - 0.6 * math.exp(-0.3 * i)
            qp, kp, vp, kpb, vpt = _qkv_call(xp, mod, norm_g, attn_w_qkv, cs_p, kv_p, layer=i, a=a,
                                             prompt=True, **prm)
            kv_p = (kp, vp)
            op = _attn_call(qp, kpb, vpt, attn_lambda[a], attn_subln_g[a], n_seq=nb, seq_len=t,
                            lam_init=lam_init)
            xp = _proj_call(op, attn_w_o, xp, mod, layer=i, w_index=a, tn=512, **prm)
            qs, ks_, vs_ = _qkv_call(xs, mod_s, norm_g, attn_w_qkv, cs_s, kv_s, layer=i, a=a,
                                     prompt=False, **srm)
            kv_s = (ks_, vs_)
            os_ = _decode_call(page_table, qs, ks_[a], vs_[a], attn_lambda[a], attn_subln_g[a],
                               cache_k, cache_v, a=a, lam_init=lam_init)
            xs = _proj_call(os_, attn_w_o, xs, mod_s, layer=i, w_index=a, tn=512, **srm)
        else:
            g = i // 2
            up, vp_, mean_p, rstd_p = _sgu_in_call(xp, mod, norm_g, sgu_w_in, layer=i, g=g,
                                                   row_chunk=(256, 512)[g % 2], **prm)
            (xp,) = _sgu_out_call(up, vp_, mean_p, rstd_p, sgu_ln_g, sgu_ln_b, sgu_w_s, sgu_b_s,
                                  sgu_w_out, xp, mod, layer=i, g=g, chunk_rows=min(CHUNK, t),
                                  want_vn=False, row_chunk=(256, 512)[g % 2], **prm)
            us, vs2, mean_s, rstd_s = _sgu_in_call(xs, mod_s, norm_g, sgu_w_in, layer=i, g=g, **srm)
            xs, vn_s = _sgu_out_call(us, vs2, mean_s, rstd_s, sgu_ln_g, sgu_ln_b, sgu_w_s, sgu_b_s,
                                     sgu_w_out, xs, mod_s, layer=i, g=g, chunk_rows=ts,
                                     want_vn=True, **srm)
            new_sgu_v.append(vn_s)
        xp = ffn(xp, mod, sub=2, which=1, final_norm=last, **prm)
        xs = ffn(xs, mod_s, sub=2, which=1, final_norm=last, **srm)

    n_sub = 2 * (d // V_DIM)
    y_prompt = xp.reshape(nb, t, d)
    y_sample = xs.reshape(db, ts, d)
    k_prompt = kv_p[0].reshape(-1, nb, t, n_sub, HEAD_DIM)
    v_prompt = kv_p[1].reshape(-1, nb, t, n_sub // 2, V_DIM)
    k_sample = kv_s[0].reshape(-1, db, ts, n_sub, HEAD_DIM)
    v_sample = kv_s[1].reshape(-1, db, ts, n_sub // 2, V_DIM)
    sgu_v_sample = jnp.stack(new_sgu_v).reshape(len(new_sgu_v), db, ts, -1)
    return (y_prompt, y_sample, k_prompt, v_prompt, k_sample, v_sample, sgu_v_sample)
```

```python
import functools
import math

import jax
import jax.numpy as jnp
from jax import lax
from jax.experimental import pallas as pl
from jax.experimental.pallas import tpu as pltpu

F32 = jnp.float32
BF16 = jnp.bfloat16

EPS = 1e-6
ROPE_THETA = 10000.0
HEAD_DIM = 128
V_DIM = 2 * HEAD_DIM
Q_SCALE = HEAD_DIM ** -0.5 * math.log2(math.e)
CHUNK = 128
SGU_GROUPS = 8
PAGE_SIZE = 128
N_MOD = 9
N_SEQ_PAD = 16
BIG = 1e30

VMEM_LIMIT_BYTES = 62 * 1024 * 1024
ROW_TILE = 1024
ROW_CHUNK = 256
WIDE_ROW_CHUNK = 1024
ATT_BLOCK = 256
ATT_QUERY_BLOCK = 512
ATT_HEADS_PER_STEP = 2
PAGES_PER_STEP = 8


def _cparams(sem):
    return pltpu.CompilerParams(dimension_semantics=sem, vmem_limit_bytes=VMEM_LIMIT_BYTES)


def _dot(a, b):
    return jnp.dot(a, b, preferred_element_type=F32)


def _dot_nt(a, b):
    return lax.dot_general(a, b, (((1,), (1,)), ((), ())), preferred_element_type=F32)


def _dot_tn(a, b):
    return lax.dot_general(a, b, (((0,), (0,)), ((), ())), preferred_element_type=F32)


def _rms(x, g):
    return x * lax.rsqrt(jnp.mean(x * x, axis=-1, keepdims=True) + EPS) * g


def _mod_rows(mod_ref, k, seq, per_row, rows):
    if per_row:
        return mod_ref[k, rows, :]
    return mod_ref[k, pl.ds(seq, 1), :]


def _modulated(x_ref, mod_ref, g_ref, seq, per_row, rows):
    shift = _mod_rows(mod_ref, 0, seq, per_row, rows)
    scale = _mod_rows(mod_ref, 1, seq, per_row, rows)
    return (_rms(x_ref[rows, :], g_ref[...]) * (1.0 + scale) + shift).astype(BF16)


def _ada_kernel(c_ref, w_ref, b_ref, o_ref):
    c = c_ref[...]
    a = (c * jax.nn.sigmoid(c)).astype(BF16)
    o_ref[...] = _dot(a, w_ref[...].astype(BF16)) + b_ref[...]


def _ada_call(c_all, w_ada, b_ada):
    depth, d, _ = w_ada.shape
    tn = 1024
    nt = d // tn
    b4 = b_ada.reshape(depth, N_MOD, 1, d)
    return pl.pallas_call(
        _ada_kernel,
        grid=(depth, N_MOD * nt),
        in_specs=[
            pl.BlockSpec((N_SEQ_PAD, d), lambda l, n: (0, 0)),
            pl.BlockSpec((None, d, tn), lambda l, n: (l, 0, n)),
            pl.BlockSpec((None, None, 1, tn), lambda l, n: (l, n // nt, 0, n % nt)),
        ],
        out_specs=pl.BlockSpec((None, None, N_SEQ_PAD, tn), lambda l, n: (l, n // nt, 0, n % nt)),
        out_shape=jax.ShapeDtypeStruct((depth, N_MOD, N_SEQ_PAD, d), F32),
        compiler_params=_cparams(("arbitrary", "arbitrary")),
        name="adaln",
    )(c_all, w_ada, b4)


def _ffn_kernel(x_ref, mod_ref, g_ref, wg_ref, wu_ref, wo_ref, fg_ref, o_ref, h_ref, *,
                per_row, tiles_per_seq, final_norm, row_chunk):
    i = pl.program_id(0)
    j = pl.program_id(1)
    nj = pl.num_programs(1)
    seq = i // tiles_per_seq
    wg = wg_ref[...]
    wu = wu_ref[...]
    wo = wo_ref[...]

    def step(first, last):
        for r in range(0, h_ref.shape[0], row_chunk):
            rows = slice(r, r + row_chunk)
            if first:
                h = _modulated(x_ref, mod_ref, g_ref, seq, per_row, rows)
                h_ref[rows, :] = h
            else:
                h = h_ref[rows, :]
            gate = _dot(h, wg)
            up = _dot(h, wu)
            a = (gate * jax.nn.sigmoid(gate) * up).astype(BF16)
            tot = _dot(a, wo)
            if not first:
                tot = o_ref[rows, :] + tot
            if last:
                gmod = _mod_rows(mod_ref, 2, seq, per_row, rows)
                tot = x_ref[rows, :] + 0.5 * gmod * tot
                if final_norm:
                    tot = _rms(tot, fg_ref[...])
            o_ref[rows, :] = tot

    @pl.when(j == 0)
    def _():
        step(True, False)

    @pl.when((j > 0) & (j < nj - 1))
    def _():
        step(False, False)

    @pl.when(j == nj - 1)
    def _():
        step(False, True)


def _ffn_call(x, mod, norm_g, w_in, w_out, final_g, *, layer, sub, which, tm, per_row,
              tiles_per_seq, final_norm):
    m, d = x.shape
    d_ff = w_out.shape[2]
    tf = 512
    nj = d_ff // tf
    assert nj >= 2
    rows = mod.shape[2]
    g3 = norm_g.reshape(norm_g.shape[0], norm_g.shape[1], 1, d)
    kern = functools.partial(_ffn_kernel, per_row=per_row, tiles_per_seq=tiles_per_seq,
                             final_norm=final_norm, row_chunk=min(tm, WIDE_ROW_CHUNK))
    x_mode = dict(pipeline_mode=pl.Buffered(1)) if final_norm else {}
    return pl.pallas_call(
        kern,
        grid=(m // tm, nj),
        in_specs=[
            pl.BlockSpec((tm, d), lambda i, j: (i, 0), **x_mode),
            pl.BlockSpec((None, 3, rows, d), lambda i, j: (layer, sub, 0, 0)),
            pl.BlockSpec((None, None, 1, d), lambda i, j: (layer, sub, 0, 0)),
            pl.BlockSpec((None, None, d, tf), lambda i, j: (layer, which, 0, j)),
            pl.BlockSpec((None, None, d, tf), lambda i, j: (layer, which, 0, nj + j)),
            pl.BlockSpec((None, None, tf, d), lambda i, j: (layer, which, j, 0)),
            pl.BlockSpec((1, d), lambda i, j: (0, 0)),
        ],
        out_specs=pl.BlockSpec((tm, d), lambda i, j: (i, 0)),
        out_shape=jax.ShapeDtypeStruct((m, d), F32),
        scratch_shapes=[pltpu.VMEM((tm, d), BF16)],
        compiler_params=_cparams(("arbitrary", "arbitrary")),
        name="ffn_half_step",
    )(x, mod, g3, w_in, w_in, w_out, final_g.reshape(1, d))


def _qkv_kernel(x_ref, mod_ref, g_ref, w_ref, cs_ref, *rest, per_row, tiles_per_seq, tn, prompt,
                aliased, row_chunk):
    if aliased:
        rest = rest[2:]
    if prompt:
        q_ref, k_ref, v_ref, kb_ref, vt_ref, h_ref = rest
    else:
        q_ref, k_ref, v_ref, h_ref = rest
    i = pl.program_id(0)
    p = pl.program_id(1)
    n = pl.program_id(2)
    seq = i // tiles_per_seq
    w = w_ref[...]

    def step(part, first):
        for ci, r in enumerate(range(0, h_ref.shape[0], row_chunk)):
            rows = slice(r, r + row_chunk)
            if first:
                h = _modulated(x_ref, mod_ref, g_ref, seq, per_row, rows)
                h_ref[rows, :] = h
            else:
                h = h_ref[rows, :]
            acc = _dot(h, w)
            if part == 2:
                v_ref[rows, :] = acc
                if prompt:
                    vt_ref[ci] = acc.T.astype(BF16)
                continue
            cos2 = cs_ref[rows, :HEAD_DIM]
            sin2 = cs_ref[rows, HEAD_DIM:]
            for hh in range(tn // HEAD_DIM):
                sl = slice(hh * HEAD_DIM, (hh + 1) * HEAD_DIM)
                xh = acc[:, sl]
                rot = xh * cos2 + pltpu.roll(xh, HEAD_DIM // 2, 1) * sin2
                if part == 0:
                    q_ref[rows, sl] = (rot * Q_SCALE).astype(q_ref.dtype)
                else:
                    k_ref[rows, sl] = rot
                    if prompt:
                        kb_ref[rows, sl] = rot.astype(BF16)

    @pl.when((p == 0) & (n == 0))
    def _():
        step(0, True)

    @pl.when((p == 0) & (n > 0))
    def _():
        step(0, False)

    @pl.when(p == 1)
    def _():
        step(1, False)

    @pl.when(p == 2)
    def _():
        step(2, False)


def _qkv_call(x, mod, norm_g, w_qkv, cs, kv_prev, *, layer, a, tm, per_row, tiles_per_seq, prompt):
    m, d = x.shape
    n_attn = w_qkv.shape[0]
    tn = 512
    nt = d // tn
    rows = mod.shape[2]
    g3 = norm_g.reshape(norm_g.shape[0], norm_g.shape[1], 1, d)
    n_cs_tiles = cs.shape[0] // tm
    row_chunk = min(tm, ATT_BLOCK)
    aliased = kv_prev is not None
    kern = functools.partial(_qkv_kernel, per_row=per_row, tiles_per_seq=tiles_per_seq, tn=tn,
                             prompt=prompt, aliased=aliased, row_chunk=row_chunk)

    def part_col(part, p, n):
        return jnp.where(p < part, 0, jnp.where(p == part, n, nt - 1))

    def out_block(part):
        return pl.BlockSpec((tm, tn), lambda i, p, n: (i, part_col(part, p, n)))

    def stacked_block(part):
        return pl.BlockSpec((None, tm, tn), lambda i, p, n: (a, i, part_col(part, p, n)))

    in_specs = [
        pl.BlockSpec((tm, d), lambda i, p, n: (i, 0)),
        pl.BlockSpec((None, 3, rows, d), lambda i, p, n: (layer, 1, 0, 0)),
        pl.BlockSpec((None, None, 1, d), lambda i, p, n: (layer, 1, 0, 0)),
        pl.BlockSpec((None, d, tn), lambda i, p, n: (a, 0, p * nt + n)),
        pl.BlockSpec((tm, 2 * HEAD_DIM), lambda i, p, n: (i % n_cs_tiles, 0)),
    ]
    args = [x, mod, g3, w_qkv, cs]
    aliases = {}
    if aliased:
        in_specs += [pl.BlockSpec(memory_space=pl.ANY), pl.BlockSpec(memory_space=pl.ANY)]
        args += list(kv_prev)
        aliases = {5: 1, 6: 2}
    out_specs = [out_block(0), stacked_block(1), stacked_block(2)]
    out_shape = [
        jax.ShapeDtypeStruct((m, d), BF16 if prompt else F32),
        jax.ShapeDtypeStruct((n_attn, m, d), F32),
        jax.ShapeDtypeStruct((n_attn, m, d), F32),
    ]
    if prompt:
        nb = tm // ATT_BLOCK
        out_specs += [out_block(1),
                      pl.BlockSpec((nb, tn, ATT_BLOCK), lambda i, p, n: (i, part_col(2, p, n), 0))]
        out_shape += [jax.ShapeDtypeStruct((m, d), BF16),
                      jax.ShapeDtypeStruct((m // ATT_BLOCK, d, ATT_BLOCK), BF16)]
    return pl.pallas_call(
        kern,
        grid=(m // tm, 3, nt),
        in_specs=in_specs,
        out_specs=out_specs,
        out_shape=out_shape,
        input_output_aliases=aliases,
        scratch_shapes=[pltpu.VMEM((tm, d), BF16)],
        compiler_params=_cparams(("arbitrary", "arbitrary", "arbitrary")),
        name="attn_qkv",
    )(*args)


def _lambda_full(lam_ref, lam_init):
    lv = lam_ref[...]
    d1 = jnp.sum(lv[0:1] * lv[1:2], axis=-1, keepdims=True)
    d2 = jnp.sum(lv[2:3] * lv[3:4], axis=-1, keepdims=True)
    return jnp.exp(d1) - jnp.exp(d2) + lam_init


def _attn_kernel(q_ref, k_ref, vt_ref, lam_ref, sg_ref, o_ref, acc_ref, m_ref, l_ref, *,
                 tq, blk, heads, lam_init):
    qi = pl.program_id(2)
    kq = tq // blk
    acc_ref[...] = jnp.zeros(acc_ref.shape, F32)
    m_ref[...] = jnp.full(m_ref.shape, -BIG, F32)
    l_ref[...] = jnp.zeros(l_ref.shape, F32)

    def scores(g, j):
        start = pl.multiple_of(j * tq, tq)
        c0 = g * V_DIM
        q1 = q_ref[:, c0:c0 + HEAD_DIM]
        q2 = q_ref[:, c0 + HEAD_DIM:c0 + V_DIM]
        k1 = k_ref[pl.ds(start, tq), c0:c0 + HEAD_DIM]
        k2 = k_ref[pl.ds(start, tq), c0 + HEAD_DIM:c0 + V_DIM]
        return jnp.concatenate([_dot_nt(k1, q1), _dot_nt(k2, q2)], axis=1)

    def update(g, s, j, diagonal):
        vt = jnp.concatenate([vt_ref[j * kq + d, g * V_DIM:(g + 1) * V_DIM, :] for d in range(kq)],
                             axis=1)
        if diagonal:
            key = lax.broadcasted_iota(jnp.int32, s.shape, 0)
            qry = lax.broadcasted_iota(jnp.int32, s.shape, 1) % tq
            s = jnp.where(key <= qry, s, -BIG)
        m_old = m_ref[g]
        m_new = jnp.maximum(m_old, jnp.max(s, axis=0, keepdims=True))
        p = jnp.exp2(s - m_new)
        alpha = jnp.exp2(m_old - m_new)
        m_ref[g] = m_new
        l_ref[g] = alpha * l_ref[g] + jnp.sum(p, axis=0, keepdims=True)
        acc_ref[g] = alpha * acc_ref[g] + _dot(vt, p.astype(BF16))

    def block(j, diagonal):
        ss = [scores(g, j) for g in range(heads)]
        for g in range(heads):
            update(g, ss[g], j, diagonal)

    def earlier_keys(j, carry):
        block(j, False)
        return carry

    lax.fori_loop(0, qi, earlier_keys, 0)
    block(qi, True)

    lam = _lambda_full(lam_ref, lam_init)
    for g in range(heads):
        on = acc_ref[g] / l_ref[g]
        o = on[:, :tq] - lam * on[:, tq:]
        o = o * lax.rsqrt(jnp.mean(o * o, axis=0, keepdims=True) + EPS) * sg_ref[...]
        o_ref[:, g * V_DIM:(g + 1) * V_DIM] = (o * (1.0 - lam_init)).T.astype(o_ref.dtype)


def _attn_call(q, kb, vt, lam_vecs, subln_g, *, n_seq, seq_len, lam_init):
    m, d = q.shape
    n_heads = d // V_DIM
    heads = ATT_HEADS_PER_STEP
    width = heads * V_DIM
    blk = ATT_BLOCK
    tq = min(ATT_QUERY_BLOCK, seq_len)
    nq = seq_len // tq
    kern = functools.partial(_attn_kernel, tq=tq, blk=blk, heads=heads, lam_init=lam_init)
    return pl.pallas_call(
        kern,
        grid=(n_seq, n_heads // heads, nq),
        in_specs=[
            pl.BlockSpec((tq, width), lambda b, h, qi: (b * nq + qi, h)),
            pl.BlockSpec((seq_len, width), lambda b, h, qi: (b, h)),
            pl.BlockSpec((seq_len // blk, width, blk), lambda b, h, qi: (b, h, 0)),
            pl.BlockSpec((4, HEAD_DIM), lambda b, h, qi: (0, 0)),
            pl.BlockSpec((V_DIM, 1), lambda b, h, qi: (0, 0)),
        ],
        out_specs=pl.BlockSpec((tq, width), lambda b, h, qi: (b * nq + qi, h)),
        out_shape=jax.ShapeDtypeStruct((m, d), BF16),
        scratch_shapes=[pltpu.VMEM((heads, V_DIM, 2 * tq), F32), pltpu.VMEM((heads, 1, 2 * tq), F32),
                        pltpu.VMEM((heads, 1, 2 * tq), F32)],
        compiler_params=_cparams(("arbitrary", "arbitrary", "arbitrary")),
        name="prompt_diff_attn",
    )(q, kb, vt, lam_vecs, subln_g.reshape(V_DIM, 1))


def _decode_kernel(pt_ref, qt_ref, kn_ref, vn_ref, lam_ref, sg_ref, *rest, n_heads, ts, lam_init):
    del pt_ref
    npg = PAGES_PER_STEP
    k_refs = rest[:npg]
    v_refs = rest[npg:2 * npg]
    o_ref, m_ref, l_ref, acc_ref = rest[2 * npg:2 * npg + 4]
    p_refs = rest[2 * npg + 4:]
    c = pl.program_id(1)
    nc = pl.num_programs(1)
    n_sub = 2 * n_heads
    cols = n_sub * ts
    keys = PAGE_SIZE

    hrow = lax.broadcasted_iota(jnp.int32, (n_sub, cols), 0)
    hlane = lax.broadcasted_iota(jnp.int32, (n_sub, cols), 1) // ts
    valid = hrow == hlane

    def to_col(a):
        row = jnp.sum(jnp.where(valid, a, 0.0), axis=0, keepdims=True)
        ri = lax.broadcasted_iota(jnp.int32, (cols, cols), 0)
        ci = lax.broadcasted_iota(jnp.int32, (cols, cols), 1)
        return jnp.sum(jnp.where(ri == ci, jnp.broadcast_to(row, (cols, cols)), 0.0), axis=1, keepdims=True)

    @pl.when(c == 0)
    def _():
        m_ref[...] = jnp.where(valid, -BIG, BIG)
        l_ref[...] = jnp.zeros(l_ref.shape, F32)
        acc_ref[...] = jnp.zeros(acc_ref.shape, F32)

    qt2 = qt_ref[...].astype(BF16)

    def flat_k(ref):
        return ref[...].reshape(keys * n_sub, HEAD_DIM).astype(BF16)

    def flat_v(ref):
        return ref[...].reshape(keys * n_heads, V_DIM).astype(BF16)

    def accumulate(state, r_list, v_list, scratch):
        m_old, l_old, acc = state
        m_new = m_old
        for r in r_list:
            m_new = jnp.maximum(m_new, jnp.max(r, axis=0))
        alpha = jnp.exp2(m_old - m_new)
        l_new = alpha * l_old
        o = None
        for r, v, p_ref in zip(r_list, v_list, scratch):
            p = jnp.exp2(r - m_new[None])
            l_new = l_new + jnp.sum(p, axis=0)
            p_ref[...] = p.reshape(keys * n_sub, cols)
            ps = (p_ref[pl.ds(0, keys * n_heads, stride=2), :]
                  + p_ref[pl.ds(1, keys * n_heads, stride=2), :]).astype(BF16)
            part = _dot_tn(ps, v)
            o = part if o is None else o + part
        return m_new, l_new, to_col(alpha) * acc + o

    def pair_scores(u):
        k2 = jnp.concatenate([flat_k(k_refs[u]), flat_k(k_refs[u + 1])], axis=1)
        r2 = _dot(k2, qt2)
        return [r2[:, :cols].reshape(keys, n_sub, cols), r2[:, cols:].reshape(keys, n_sub, cols)]

    state = (m_ref[...], l_ref[...], acc_ref[...])
    r_next = pair_scores(0)
    for u in range(0, npg, 2):
        r_pair = r_next
        if u + 2 < npg:
            r_next = pair_scores(u + 2)
        slot = u % 4
        state = accumulate(state, r_pair, [flat_v(v_refs[u]), flat_v(v_refs[u + 1])],
                           p_refs[slot:slot + 2])
    m_ref[...], l_ref[...], acc_ref[...] = state

    @pl.when(c == nc - 1)
    def _():
        r = _dot(kn_ref[...].astype(BF16), qt2[:HEAD_DIM, :cols]).reshape(keys, n_sub, cols)
        key = lax.broadcasted_iota(jnp.int32, r.shape, 0)
        qry = lax.broadcasted_iota(jnp.int32, r.shape, 2) % ts
        r = jnp.where(key <= qry, r, -BIG)
        _, l_fin, acc_fin = accumulate((m_ref[...], l_ref[...], acc_ref[...]), [r],
                                       [vn_ref[...].astype(BF16)], p_refs[:1])
        lam = _lambda_full(lam_ref, lam_init)
        on = acc_fin / to_col(l_fin)
        for hh in range(n_heads):
            o1 = on[(2 * hh) * ts:(2 * hh + 1) * ts]
            o2 = on[(2 * hh + 1) * ts:(2 * hh + 2) * ts]
            o = o1 - lam * o2
            o_ref[:, hh * V_DIM:(hh + 1) * V_DIM] = _rms(o, sg_ref[...]) * (1.0 - lam_init)


def _decode_call(page_table, q, k_new, v_new, lam_vecs, subln_g, cache_k, cache_v, *, a, lam_init):
    n_seq, n_pages = page_table.shape
    m, d = q.shape
    ts = m // n_seq
    n_heads = d // V_DIM
    n_sub = 2 * n_heads
    cols = n_sub * ts
    npg = PAGES_PER_STEP
    nc = n_pages // npg
    pad = PAGE_SIZE - ts
    qt = jnp.transpose(q.reshape(n_seq, ts, n_sub, HEAD_DIM), (0, 3, 2, 1)).reshape(n_seq, HEAD_DIM, cols)
    z = jnp.zeros_like(qt)
    qt2 = jnp.concatenate([jnp.concatenate([qt, z], axis=2), jnp.concatenate([z, qt], axis=2)], axis=1)
    kn = jnp.pad(k_new.reshape(n_seq, ts, n_sub, HEAD_DIM), ((0, 0), (0, pad), (0, 0), (0, 0)))
    vn = jnp.pad(v_new.reshape(n_seq, ts, n_heads, V_DIM), ((0, 0), (0, pad), (0, 0), (0, 0)))
    kn = kn.reshape(n_seq, PAGE_SIZE * n_sub, HEAD_DIM)
    vn = vn.reshape(n_seq, PAGE_SIZE * n_heads, V_DIM)

    def page_spec(shape, u):
        return pl.BlockSpec((None, None) + shape,
                            lambda b, c, pt: (a, pt[b, c * npg + u], 0, 0, 0))

    k_specs = [page_spec(cache_k.shape[2:], u) for u in range(npg)]
    v_specs = [page_spec(cache_v.shape[2:], u) for u in range(npg)]
    kern = functools.partial(_decode_kernel, n_heads=n_heads, ts=ts, lam_init=lam_init)
    grid_spec = pltpu.PrefetchScalarGridSpec(
        num_scalar_prefetch=1,
        grid=(n_seq, nc),
        in_specs=[
            pl.BlockSpec((None, 2 * HEAD_DIM, 2 * cols), lambda b, c, pt: (b, 0, 0)),
            pl.BlockSpec((None, PAGE_SIZE * n_sub, HEAD_DIM), lambda b, c, pt: (b, 0, 0)),
            pl.BlockSpec((None, PAGE_SIZE * n_heads, V_DIM), lambda b, c, pt: (b, 0, 0)),
            pl.BlockSpec((4, HEAD_DIM), lambda b, c, pt: (0, 0)),
            pl.BlockSpec((1, V_DIM), lambda b, c, pt: (0, 0)),
        ] + k_specs + v_specs,
        out_specs=pl.BlockSpec((ts, d), lambda b, c, pt: (b, 0)),
        scratch_shapes=[
            pltpu.VMEM((n_sub, cols), F32),
            pltpu.VMEM((n_sub, cols), F32),
            pltpu.VMEM((cols, V_DIM), F32),
        ] + [pltpu.VMEM((PAGE_SIZE * n_sub, cols), F32) for _ in range(4)],
    )
    return pl.pallas_call(
        kern,
        grid_spec=grid_spec,
        out_shape=jax.ShapeDtypeStruct((m, d), F32),
        compiler_params=_cparams(("arbitrary", "arbitrary")),
        name="sample_paged_diff_attn",
    )(page_table, qt2, kn, vn, lam_vecs, subln_g.reshape(1, V_DIM),
      *([cache_k] * npg), *([cache_v] * npg))


def _proj_kernel(a_ref, w_ref, x_ref, mod_ref, o_ref, *, per_row, tiles_per_seq):
    seq = pl.program_id(0) // tiles_per_seq
    mix = _dot(a_ref[...].astype(BF16), w_ref[...])
    gmod = _mod_rows(mod_ref, 2, seq, per_row, slice(None))
    o_ref[...] = x_ref[...] + gmod * mix


def _proj_call(act, w, x, mod, *, layer, w_index, tm, tn, per_row, tiles_per_seq):
    m, d = x.shape
    kdim = act.shape[1]
    rows = mod.shape[2]
    kern = functools.partial(_proj_kernel, per_row=per_row, tiles_per_seq=tiles_per_seq)
    return pl.pallas_call(
        kern,
        grid=(m // tm, d // tn),
        in_specs=[
            pl.BlockSpec((tm, kdim), lambda i, n: (i, 0)),
            pl.BlockSpec((None, kdim, tn), lambda i, n: (w_index, 0, n)),
            pl.BlockSpec((tm, tn), lambda i, n: (i, n)),
            pl.BlockSpec((None, 3, rows, tn), lambda i, n: (layer, 1, 0, n)),
        ],
        out_specs=pl.BlockSpec((tm, tn), lambda i, n: (i, n)),
        out_shape=jax.ShapeDtypeStruct((m, d), F32),
        compiler_params=_cparams(("arbitrary", "arbitrary")),
        name="gated_residual_proj",
    )(act, w, x, mod)


def _sgu_in_kernel(x_ref, mod_ref, g_ref, w_ref, u_ref, v_ref, mean_ref, rstd_ref,
                   h_ref, s1_ref, s2_ref, *, per_row, tiles_per_seq, half, row_chunk):
    i = pl.program_id(0)
    n = pl.program_id(1)
    nn = pl.num_programs(1)
    nu = nn // 2
    seq = i // tiles_per_seq
    w = w_ref[...]

    def step(first, is_v):
        for r in range(0, h_ref.shape[0], row_chunk):
            rows = slice(r, r + row_chunk)
            if first:
                h = _modulated(x_ref, mod_ref, g_ref, seq, per_row, rows)
                h_ref[rows, :] = h
            else:
                h = h_ref[rows, :]
            z = _dot(h, w)
            z = 0.5 * z * (1.0 + lax.erf(z * (2.0 ** -0.5)))
            if is_v:
                v_ref[rows, :] = z
                s1_ref[rows, :] += jnp.sum(z, axis=-1, keepdims=True)
                s2_ref[rows, :] += jnp.sum(z * z, axis=-1, keepdims=True)
            else:
                u_ref[rows, :] = z.astype(u_ref.dtype)

    @pl.when(n == 0)
    def _():
        s1_ref[...] = jnp.zeros(s1_ref.shape, F32)
        s2_ref[...] = jnp.zeros(s2_ref.shape, F32)
        step(True, False)

    @pl.when((n > 0) & (n < nu))
    def _():
        step(False, False)

    @pl.when(n >= nu)
    def _():
        step(False, True)

    @pl.when(n == nn - 1)
    def _():
        mean = s1_ref[...] * (1.0 / half)
        var = s2_ref[...] * (1.0 / half) - mean * mean
        mean_ref[...] = mean
        rstd_ref[...] = lax.rsqrt(var + EPS)


def _sgu_in_call(x, mod, norm_g, w_in, *, layer, g, tm, per_row, tiles_per_seq):
    m, d = x.shape
    ffn = w_in.shape[2]
    half = ffn // 2
    tn = 512
    nn = ffn // tn
    nu = nn // 2
    assert nu >= 2
    rows = mod.shape[2]
    g3 = norm_g.reshape(norm_g.shape[0], norm_g.shape[1], 1, d)
    kern = functools.partial(_sgu_in_kernel, per_row=per_row, tiles_per_seq=tiles_per_seq, half=half,
                             row_chunk=min(tm, WIDE_ROW_CHUNK))
    return pl.pallas_call(
        kern,
        grid=(m // tm, nn),
        in_specs=[
            pl.BlockSpec((tm, d), lambda i, n: (i, 0)),
            pl.BlockSpec((None, 3, rows, d), lambda i, n: (layer, 1, 0, 0)),
            pl.BlockSpec((None, None, 1, d), lambda i, n: (layer, 1, 0, 0)),
            pl.BlockSpec((None, d, tn), lambda i, n: (g, 0, n)),
        ],
        out_specs=[
            pl.BlockSpec((tm, tn), lambda i, n: (i, jnp.minimum(n, nu - 1))),
            pl.BlockSpec((tm, tn), lambda i, n: (i, jnp.maximum(n - nu, 0))),
            pl.BlockSpec((tm, 1), lambda i, n: (i, 0)),
            pl.BlockSpec((tm, 1), lambda i, n: (i, 0)),
        ],
        out_shape=[
            jax.ShapeDtypeStruct((m, half), BF16),
            jax.ShapeDtypeStruct((m, half), F32),
            jax.ShapeDtypeStruct((m, 1), F32),
            jax.ShapeDtypeStruct((m, 1), F32),
        ],
        scratch_shapes=[pltpu.VMEM((tm, d), BF16), pltpu.VMEM((tm, 1), F32), pltpu.VMEM((tm, 1), F32)],
        compiler_params=_cparams(("arbitrary", "arbitrary")),
        name="sgu_in",
    )(x, mod, g3, w_in)


def _sgu_out_kernel(u_ref, v_ref, mean_ref, rstd_ref, lg_ref, lb_ref, ws_ref, bs_ref, w_ref, x_ref,
                    mod_ref, o_ref, vn_ref, *, per_row, tiles_per_seq, chunk_rows, row_chunk):
    i = pl.program_id(0)
    j = pl.program_id(1)
    nj = pl.num_programs(1)
    seq = i // tiles_per_seq
    cr = chunk_rows
    r = lax.broadcasted_iota(jnp.int32, (cr, cr), 0)
    c = lax.broadcasted_iota(jnp.int32, (cr, cr), 1)
    ws = jnp.where(c <= r, ws_ref[:cr, :cr], 0.0).astype(BF16)
    bs = bs_ref[:cr, :]
    w = w_ref[...]

    def step(first, last):
        for r0 in range(0, u_ref.shape[0], row_chunk):
            rows = slice(r0, r0 + row_chunk)
            gated = []
            for c0 in range(r0, r0 + row_chunk, cr):
                sl = slice(c0, c0 + cr)
                vn = (v_ref[sl, :] - mean_ref[sl, :]) * rstd_ref[sl, :] * lg_ref[...] + lb_ref[...]
                if vn_ref is not None:
                    vn_ref[sl, :] = vn
                s = _dot(ws, vn.astype(BF16)) + bs
                gated.append(u_ref[sl, :].astype(F32) * s)
            tot = _dot(jnp.concatenate(gated, axis=0).astype(BF16), w)
            if not first:
                tot = o_ref[rows, :] + tot
            if last:
                tot = x_ref[rows, :] + _mod_rows(mod_ref, 2, seq, per_row, rows) * tot
            o_ref[rows, :] = tot

    @pl.when(j == 0)
    def _():
        step(True, False)

    @pl.when((j > 0) & (j < nj - 1))
    def _():
        step(False, False)

    @pl.when(j == nj - 1)
    def _():
        step(False, True)


def _sgu_out_call(u, v, mean, rstd, ln_g, ln_b, w_s, b_s, w_out, x, mod, *, layer, g, tm, per_row,
                  tiles_per_seq, chunk_rows, want_vn):
    m, half = u.shape
    d = x.shape[1]
    gd = half // SGU_GROUPS
    rows = mod.shape[2]
    bs3 = b_s.reshape(b_s.shape[0], SGU_GROUPS, CHUNK, 1)
    lg3 = ln_g.reshape(ln_g.shape[0], 1, half)
    lb3 = ln_b.reshape(ln_b.shape[0], 1, half)
    params = dict(per_row=per_row, tiles_per_seq=tiles_per_seq, chunk_rows=chunk_rows,
                  row_chunk=min(tm, ROW_CHUNK))

    def kern(*refs):
        if want_vn:
            _sgu_out_kernel(*refs, **params)
        else:
            _sgu_out_kernel(*refs, None, **params)

    blk = pl.BlockSpec((tm, gd), lambda i, j: (i, j))
    out_specs = [pl.BlockSpec((tm, d), lambda i, j: (i, 0))]
    out_shape = [jax.ShapeDtypeStruct((m, d), F32)]
    if want_vn:
        out_specs.append(blk)
        out_shape.append(jax.ShapeDtypeStruct((m, half), F32))
    return pl.pallas_call(
        kern,
        grid=(m // tm, SGU_GROUPS),
        in_specs=[
            blk,
            blk,
            pl.BlockSpec((tm, 1), lambda i, j: (i, 0)),
            pl.BlockSpec((tm, 1), lambda i, j: (i, 0)),
            pl.BlockSpec((None, 1, gd), lambda i, j: (g, 0, j)),
            pl.BlockSpec((None, 1, gd), lambda i, j: (g, 0, j)),
            pl.BlockSpec((None, None, CHUNK, CHUNK), lambda i, j: (g, j, 0, 0)),
            pl.BlockSpec((None, None, CHUNK, 1), lambda i, j: (g, j, 0, 0)),
            pl.BlockSpec((None, gd, d), lambda i, j: (g, j, 0)),
            pl.BlockSpec((tm, d), lambda i, j: (i, 0)),
            pl.BlockSpec((None, 3, rows, d), lambda i, j: (layer, 1, 0, 0)),
        ],
        out_specs=out_specs,
        out_shape=out_shape,
        compiler_params=_cparams(("arbitrary", "arbitrary")),
        name="sgu_gate_out",
    )(u, v, mean, rstd, lg3, lb3, w_s, bs3, w_out, x, mod)


def _rotary_table(pos):
    half = HEAD_DIM // 2
    inv_freq = ROPE_THETA ** (-jnp.arange(half, dtype=F32) / half)
    ang = pos.astype(F32)[:, None] * inv_freq[None, :]
    cos, sin = jnp.cos(ang), jnp.sin(ang)
    return jnp.concatenate([cos, cos, -sin, sin], axis=-1)


def kernel(x_prompt, x_sample, c_prompt, c_sample, cache_k, cache_v, page_table, norm_g, w_ada, b_ada,
           ffn_w_in, ffn_w_out, attn_w_qkv, attn_w_o, attn_lambda, attn_subln_g, sgu_w_in, sgu_ln_g,
           sgu_ln_b, sgu_w_s, sgu_b_s, sgu_w_out, final_g):
    nb, t, d = x_prompt.shape
    db, ts, _ = x_sample.shape
    depth = w_ada.shape[0]
    n_pages = page_table.shape[1]
    past = n_pages * PAGE_SIZE
    mp, ms = nb * t, db * ts
    tm = min(ROW_TILE, t)
    tps = t // tm

    ffn_w_in, ffn_w_out, attn_w_qkv, attn_w_o, sgu_w_in, sgu_w_out = (
        w.astype(BF16) for w in (ffn_w_in, ffn_w_out, attn_w_qkv, attn_w_o, sgu_w_in, sgu_w_out))

    c_all = jnp.concatenate([c_prompt, c_sample, jnp.zeros((N_SEQ_PAD - nb - db, d), F32)], axis=0)
    mod = _ada_call(c_all, w_ada, b_ada)
    mod_s = jnp.repeat(mod[:, :, nb:nb + db], ts, axis=2)

    cs_p = _rotary_table(jnp.arange(t, dtype=jnp.int32))
    cs_s = jnp.tile(_rotary_table(past + jnp.arange(ts, dtype=jnp.int32)), (db, 1))

    xp = x_prompt.reshape(mp, d)
    xs = x_sample.reshape(ms, d)
    prm = dict(tm=tm, per_row=False, tiles_per_seq=tps)
    srm = dict(tm=ms, per_row=True, tiles_per_seq=1)
    kv_p = kv_s = None
    new_sgu_v = []

    for i in range(depth):
        last = i == depth - 1
        ffn = functools.partial(_ffn_call, norm_g=norm_g, w_in=ffn_w_in, w_out=ffn_w_out,
                                final_g=final_g, layer=i)
        xp = ffn(xp, mod, sub=0, which=0, final_norm=False, **prm)
        xs = ffn(xs, mod_s, sub=0, which=0, final_norm=False, **srm)
        if i % 2 == 0:
            a = i // 2
            lam_init = 0.8 - 0.6 * math.exp(-0.3 * i)
            qp, kp, vp, kpb, vpt = _qkv_call(xp, mod, norm_g, attn_w_qkv, cs_p, kv_p, layer=i, a=a,
                                             prompt=True, **prm)
            kv_p = (kp, vp)
            op = _attn_call(qp, kpb, vpt, attn_lambda[a], attn_subln_g[a], n_seq=nb, seq_len=t,
                            lam_init=lam_init)
            xp = _proj_call(op, attn_w_o, xp, mod, layer=i, w_index=a, tn=512, **prm)
            qs, ks_, vs_ = _qkv_call(xs, mod_s, norm_g, attn_w_qkv, cs_s, kv_s, layer=i, a=a,
                                     prompt=False, **srm)
            kv_s = (ks_, vs_)
            os_ = _decode_call(page_table, qs, ks_[a], vs_[a], attn_lambda[a], attn_subln_g[a],
                               cache_k, cache_v, a=a, lam_init=lam_init)
            xs = _proj_call(os_, attn_w_o, xs, mod_s, layer=i, w_index=a, tn=512, **srm)
        else:
            g = i // 2
            up, vp_, mean_p, rstd_p = _sgu_in_call(xp, mod, norm_g, sgu_w_in, layer=i, g=g, **prm)
            (xp,) = _sgu_out_call(up, vp_, mean_p, rstd_p, sgu_ln_g, sgu_ln_b, sgu_w_s, sgu_b_s,
                                  sgu_w_out, xp, mod, layer=i, g=g, chunk_rows=min(CHUNK, t),
                                  want_vn=False, **prm)
            us, vs2, mean_s, rstd_s = _sgu_in_call(xs, mod_s, norm_g, sgu_w_in, layer=i, g=g, **srm)
            xs, vn_s = _sgu_out_call(us, vs2, mean_s, rstd_s, sgu_ln_g, sgu_ln_b, sgu_w_s, sgu_b_s,
                                     sgu_w_out, xs, mod_s, layer=i, g=g, chunk_rows=ts,
                                     want_vn=True, **srm)
            new_sgu_v.append(vn_s)
        xp = ffn(xp, mod, sub=2, which=1, final_norm=last, **prm)
        xs = ffn(xs, mod_s, sub=2, which=1, final_norm=last, **srm)

    n_sub = 2 * (d // V_DIM)
    y_prompt = xp.reshape(nb, t, d)
    y_sample = xs.reshape(db, ts, d)
    k_prompt = kv_p[0].reshape(-1, nb, t, n_sub, HEAD_DIM)
    v_prompt = kv_p[1].reshape(-1, nb, t, n_sub // 2, V_DIM)
    k_sample = kv_s[0].reshape(-1, db, ts, n_sub, HEAD_DIM)
    v_sample = kv_s[1].reshape(-1, db, ts, n_sub // 2, V_DIM)
    sgu_v_sample = jnp.stack(new_sgu_v).reshape(len(new_sgu_v), db, ts, -1)
    return (y_prompt, y_sample, k_prompt, v_prompt, k_sample, v_sample, sgu_v_sample)
```

```python
import functools
import math

import jax
import jax.numpy as jnp
from jax import lax
from jax.experimental import pallas as pl
from jax.experimental.pallas import tpu as pltpu

F32 = jnp.float32
BF16 = jnp.bfloat16

EPS = 1e-6
ROPE_THETA = 10000.0
HEAD_DIM = 128
V_DIM = 2 * HEAD_DIM
Q_SCALE = HEAD_DIM ** -0.5 * math.log2(math.e)
CHUNK = 128
SGU_GROUPS = 8
PAGE_SIZE = 128
N_MOD = 9
N_SEQ_PAD = 16
BIG = 1e30

VMEM_LIMIT_BYTES = 62 * 1024 * 1024
ROW_TILE = 1024
ROW_CHUNK = 256
WIDE_ROW_CHUNK = 1024
ATT_BLOCK = 256
ATT_QUERY_BLOCK = 512
ATT_HEADS_PER_STEP = 2
PAGES_PER_STEP = 8


def _cparams(sem):
    return pltpu.CompilerParams(dimension_semantics=sem, vmem_limit_bytes=VMEM_LIMIT_BYTES)


def _dot(a, b):
    return jnp.dot(a, b, preferred_element_type=F32)


def _dot_nt(a, b):
    return lax.dot_general(a, b, (((1,), (1,)), ((), ())), preferred_element_type=F32)


def _dot_tn(a, b):
    return lax.dot_general(a, b, (((0,), (0,)), ((), ())), preferred_element_type=F32)


def _rms(x, g):
    return x * lax.rsqrt(jnp.mean(x * x, axis=-1, keepdims=True) + EPS) * g


def _mod_rows(mod_ref, k, seq, per_row, rows):
    if per_row:
        return mod_ref[k, rows, :]
    return mod_ref[k, pl.ds(seq, 1), :]


def _modulated(x_ref, mod_ref, g_ref, seq, per_row, rows):
    shift = _mod_rows(mod_ref, 0, seq, per_row, rows)
    scale = _mod_rows(mod_ref, 1, seq, per_row, rows)
    return (_rms(x_ref[rows, :], g_ref[...]) * (1.0 + scale) + shift).astype(BF16)


def _side_cast_plan(weights, n_steps, step_of):
    in_specs, args, out_specs, out_shapes = [], [], [], []
    for w4, li, wi in weights:
        n_rows, n_cols = w4.shape[2:]
        per = next(k for k in range(1, n_steps + 1)
                   if n_steps % k == 0 and n_rows % (n_steps // k) == 0
                   and (n_rows // (n_steps // k)) % 16 == 0)
        rb = n_rows // (n_steps // per)
        in_specs.append(pl.BlockSpec((None, None, rb, n_cols),
                                     lambda *g, li=li, wi=wi, per=per: (li, wi, step_of(*g) // per, 0)))
        out_specs.append(pl.BlockSpec((rb, n_cols), lambda *g, per=per: (step_of(*g) // per, 0)))
        out_shapes.append(jax.ShapeDtypeStruct((n_rows, n_cols), BF16))
        args.append(w4)
    return in_specs, args, out_specs, out_shapes


def _side_cast(in_refs, out_refs):
    for src, dst in zip(in_refs, out_refs):
        dst[...] = src[...].astype(BF16)


def _ada_kernel(c_ref, w_ref, b_ref, o_ref):
    c = c_ref[...]
    a = (c * jax.nn.sigmoid(c)).astype(BF16)
    o_ref[...] = _dot(a, w_ref[...].astype(BF16)) + b_ref[...]


def _ada_call(c_all, w_ada, b_ada):
    depth, d, _ = w_ada.shape
    tn = 1024
    nt = d // tn
    b4 = b_ada.reshape(depth, N_MOD, 1, d)
    return pl.pallas_call(
        _ada_kernel,
        grid=(depth, N_MOD * nt),
        in_specs=[
            pl.BlockSpec((N_SEQ_PAD, d), lambda l, n: (0, 0)),
            pl.BlockSpec((None, d, tn), lambda l, n: (l, 0, n)),
            pl.BlockSpec((None, None, 1, tn), lambda l, n: (l, n // nt, 0, n % nt)),
        ],
        out_specs=pl.BlockSpec((None, None, N_SEQ_PAD, tn), lambda l, n: (l, n // nt, 0, n % nt)),
        out_shape=jax.ShapeDtypeStruct((depth, N_MOD, N_SEQ_PAD, d), F32),
        compiler_params=_cparams(("arbitrary", "arbitrary")),
        name="adaln",
    )(c_all, w_ada, b4)


def _ffn_kernel(x_ref, mod_ref, g_ref, wg_ref, wu_ref, wo_ref, fg_ref, o_ref, h_ref, *,
                per_row, tiles_per_seq, final_norm, row_chunk):
    i = pl.program_id(0)
    j = pl.program_id(1)
    nj = pl.num_programs(1)
    seq = i // tiles_per_seq
    wg = wg_ref[...]
    wu = wu_ref[...]
    wo = wo_ref[...]

    def step(first, last):
        for r in range(0, h_ref.shape[0], row_chunk):
            rows = slice(r, r + row_chunk)
            if first:
                h = _modulated(x_ref, mod_ref, g_ref, seq, per_row, rows)
                h_ref[rows, :] = h
            else:
                h = h_ref[rows, :]
            gate = _dot(h, wg)
            up = _dot(h, wu)
            a = (gate * jax.nn.sigmoid(gate) * up).astype(BF16)
            tot = _dot(a, wo)
            if not first:
                tot = o_ref[rows, :] + tot
            if last:
                gmod = _mod_rows(mod_ref, 2, seq, per_row, rows)
                tot = x_ref[rows, :] + 0.5 * gmod * tot
                if final_norm:
                    tot = _rms(tot, fg_ref[...])
            o_ref[rows, :] = tot

    @pl.when(j == 0)
    def _():
        step(True, False)

    @pl.when((j > 0) & (j < nj - 1))
    def _():
        step(False, False)

    @pl.when(j == nj - 1)
    def _():
        step(False, True)


def _ffn_call(x, mod, norm_g, w_in, w_out, final_g, *, layer, sub, tm, per_row,
              tiles_per_seq, final_norm):
    m, d = x.shape
    d_ff = w_out.shape[0]
    tf = 512
    nj = d_ff // tf
    assert nj >= 2
    rows = mod.shape[2]
    g3 = norm_g.reshape(norm_g.shape[0], norm_g.shape[1], 1, d)
    kern = functools.partial(_ffn_kernel, per_row=per_row, tiles_per_seq=tiles_per_seq,
                             final_norm=final_norm, row_chunk=min(tm, WIDE_ROW_CHUNK))
    x_mode = dict(pipeline_mode=pl.Buffered(1)) if final_norm else {}
    return pl.pallas_call(
        kern,
        grid=(m // tm, nj),
        in_specs=[
            pl.BlockSpec((tm, d), lambda i, j: (i, 0), **x_mode),
            pl.BlockSpec((None, 3, rows, d), lambda i, j: (layer, sub, 0, 0)),
            pl.BlockSpec((None, None, 1, d), lambda i, j: (layer, sub, 0, 0)),
            pl.BlockSpec((d, tf), lambda i, j: (0, j)),
            pl.BlockSpec((d, tf), lambda i, j: (0, nj + j)),
            pl.BlockSpec((tf, d), lambda i, j: (j, 0)),
            pl.BlockSpec((1, d), lambda i, j: (0, 0)),
        ],
        out_specs=pl.BlockSpec((tm, d), lambda i, j: (i, 0)),
        out_shape=jax.ShapeDtypeStruct((m, d), F32),
        scratch_shapes=[pltpu.VMEM((tm, d), BF16)],
        compiler_params=_cparams(("arbitrary", "arbitrary")),
        name="ffn_half_step",
    )(x, mod, g3, w_in, w_in, w_out, final_g.reshape(1, d))


def _qkv_kernel(x_ref, mod_ref, g_ref, w_ref, cs_ref, *rest, per_row, tiles_per_seq, tn, prompt,
                aliased, row_chunk):
    if aliased:
        rest = rest[2:]
    if prompt:
        q_ref, k_ref, v_ref, kb_ref, vt_ref, h_ref = rest
    else:
        q_ref, k_ref, v_ref, h_ref = rest
    i = pl.program_id(0)
    p = pl.program_id(1)
    n = pl.program_id(2)
    seq = i // tiles_per_seq
    w = w_ref[...]

    def step(part, first):
        for ci, r in enumerate(range(0, h_ref.shape[0], row_chunk)):
            rows = slice(r, r + row_chunk)
            if first:
                h = _modulated(x_ref, mod_ref, g_ref, seq, per_row, rows)
                h_ref[rows, :] = h
            else:
                h = h_ref[rows, :]
            acc = _dot(h, w)
            if part == 2:
                v_ref[rows, :] = acc
                if prompt:
                    vt_ref[ci] = acc.T.astype(BF16)
                continue
            cos2 = cs_ref[rows, :HEAD_DIM]
            sin2 = cs_ref[rows, HEAD_DIM:]
            for hh in range(tn // HEAD_DIM):
                sl = slice(hh * HEAD_DIM, (hh + 1) * HEAD_DIM)
                xh = acc[:, sl]
                rot = xh * cos2 + pltpu.roll(xh, HEAD_DIM // 2, 1) * sin2
                if part == 0:
                    q_ref[rows, sl] = (rot * Q_SCALE).astype(q_ref.dtype)
                else:
                    k_ref[rows, sl] = rot
                    if prompt:
                        kb_ref[rows, sl] = rot.astype(BF16)

    @pl.when((p == 0) & (n == 0))
    def _():
        step(0, True)

    @pl.when((p == 0) & (n > 0))
    def _():
        step(0, False)

    @pl.when(p == 1)
    def _():
        step(1, False)

    @pl.when(p == 2)
    def _():
        step(2, False)


def _qkv_call(x, mod, norm_g, w_qkv, cs, kv_prev, *, layer, a, tm, per_row, tiles_per_seq, prompt):
    m, d = x.shape
    n_attn = w_qkv.shape[0]
    tn = 512
    nt = d // tn
    rows = mod.shape[2]
    g3 = norm_g.reshape(norm_g.shape[0], norm_g.shape[1], 1, d)
    n_cs_tiles = cs.shape[0] // tm
    row_chunk = min(tm, ATT_BLOCK)
    aliased = kv_prev is not None
    kern = functools.partial(_qkv_kernel, per_row=per_row, tiles_per_seq=tiles_per_seq, tn=tn,
                             prompt=prompt, aliased=aliased, row_chunk=row_chunk)

    def part_col(part, p, n):
        return jnp.where(p < part, 0, jnp.where(p == part, n, nt - 1))

    def out_block(part):
        return pl.BlockSpec((tm, tn), lambda i, p, n: (i, part_col(part, p, n)))

    def stacked_block(part):
        return pl.BlockSpec((None, tm, tn), lambda i, p, n: (a, i, part_col(part, p, n)))

    in_specs = [
        pl.BlockSpec((tm, d), lambda i, p, n: (i, 0)),
        pl.BlockSpec((None, 3, rows, d), lambda i, p, n: (layer, 1, 0, 0)),
        pl.BlockSpec((None, None, 1, d), lambda i, p, n: (layer, 1, 0, 0)),
        pl.BlockSpec((None, d, tn), lambda i, p, n: (a, 0, p * nt + n)),
        pl.BlockSpec((tm, 2 * HEAD_DIM), lambda i, p, n: (i % n_cs_tiles, 0)),
    ]
    args = [x, mod, g3, w_qkv, cs]
    aliases = {}
    if aliased:
        in_specs += [pl.BlockSpec(memory_space=pl.ANY), pl.BlockSpec(memory_space=pl.ANY)]
        args += list(kv_prev)
        aliases = {5: 1, 6: 2}
    out_specs = [out_block(0), stacked_block(1), stacked_block(2)]
    out_shape = [
        jax.ShapeDtypeStruct((m, d), BF16 if prompt else F32),
        jax.ShapeDtypeStruct((n_attn, m, d), F32),
        jax.ShapeDtypeStruct((n_attn, m, d), F32),
    ]
    if prompt:
        nb = tm // ATT_BLOCK
        out_specs += [out_block(1),
                      pl.BlockSpec((nb, tn, ATT_BLOCK), lambda i, p, n: (i, part_col(2, p, n), 0))]
        out_shape += [jax.ShapeDtypeStruct((m, d), BF16),
                      jax.ShapeDtypeStruct((m // ATT_BLOCK, d, ATT_BLOCK), BF16)]
    return pl.pallas_call(
        kern,
        grid=(m // tm, 3, nt),
        in_specs=in_specs,
        out_specs=out_specs,
        out_shape=out_shape,
        input_output_aliases=aliases,
        scratch_shapes=[pltpu.VMEM((tm, d), BF16)],
        compiler_params=_cparams(("arbitrary", "arbitrary", "arbitrary")),
        name="attn_qkv",
    )(*args)


def _lambda_full(lam_ref, lam_init):
    lv = lam_ref[...]
    d1 = jnp.sum(lv[0:1] * lv[1:2], axis=-1, keepdims=True)
    d2 = jnp.sum(lv[2:3] * lv[3:4], axis=-1, keepdims=True)
    return jnp.exp(d1) - jnp.exp(d2) + lam_init


def _attn_kernel(q_ref, k_ref, vt_ref, lam_ref, sg_ref, *rest, tq, blk, heads, lam_init, n_side):
    side_in, rest = rest[:n_side], rest[n_side:]
    o_ref, side_out = rest[0], rest[1:1 + n_side]
    acc_ref, m_ref, l_ref = rest[1 + n_side:]
    _side_cast(side_in, side_out)
    qi = pl.program_id(2)
    kq = tq // blk
    acc_ref[...] = jnp.zeros(acc_ref.shape, F32)
    m_ref[...] = jnp.full(m_ref.shape, -BIG, F32)
    l_ref[...] = jnp.zeros(l_ref.shape, F32)

    def scores(g, j):
        start = pl.multiple_of(j * tq, tq)
        c0 = g * V_DIM
        q1 = q_ref[:, c0:c0 + HEAD_DIM]
        q2 = q_ref[:, c0 + HEAD_DIM:c0 + V_DIM]
        k1 = k_ref[pl.ds(start, tq), c0:c0 + HEAD_DIM]
        k2 = k_ref[pl.ds(start, tq), c0 + HEAD_DIM:c0 + V_DIM]
        return jnp.concatenate([_dot_nt(k1, q1), _dot_nt(k2, q2)], axis=1)

    def update(g, s, j, diagonal):
        vt = jnp.concatenate([vt_ref[j * kq + d, g * V_DIM:(g + 1) * V_DIM, :] for d in range(kq)],
                             axis=1)
        if diagonal:
            key = lax.broadcasted_iota(jnp.int32, s.shape, 0)
            qry = lax.broadcasted_iota(jnp.int32, s.shape, 1) % tq
            s = jnp.where(key <= qry, s, -BIG)
        m_old = m_ref[g]
        m_new = jnp.maximum(m_old, jnp.max(s, axis=0, keepdims=True))
        p = jnp.exp2(s - m_new)
        alpha = jnp.exp2(m_old - m_new)
        m_ref[g] = m_new
        l_ref[g] = alpha * l_ref[g] + jnp.sum(p, axis=0, keepdims=True)
        acc_ref[g] = alpha * acc_ref[g] + _dot(vt, p.astype(BF16))

    def block(j, diagonal):
        ss = [scores(g, j) for g in range(heads)]
        for g in range(heads):
            update(g, ss[g], j, diagonal)

    def earlier_keys(j, carry):
        block(j, False)
        return carry

    lax.fori_loop(0, qi, earlier_keys, 0)
    block(qi, True)

    lam = _lambda_full(lam_ref, lam_init)
    for g in range(heads):
        on = acc_ref[g] / l_ref[g]
        o = on[:, :tq] - lam * on[:, tq:]
        o = o * lax.rsqrt(jnp.mean(o * o, axis=0, keepdims=True) + EPS) * sg_ref[...]
        o_ref[:, g * V_DIM:(g + 1) * V_DIM] = (o * (1.0 - lam_init)).T.astype(o_ref.dtype)


def _attn_call(q, kb, vt, lam_vecs, subln_g, cast_weights, *, n_seq, seq_len, lam_init,
               heads=ATT_HEADS_PER_STEP):
    m, d = q.shape
    n_heads = d // V_DIM
    width = heads * V_DIM
    blk = ATT_BLOCK
    tq = min(ATT_QUERY_BLOCK, seq_len)
    nq = seq_len // tq
    nh = n_heads // heads
    side_in, side_args, side_out, side_shapes = _side_cast_plan(
        cast_weights, n_seq * nh * nq, lambda b, h, qi: (b * nh + h) * nq + qi)
    kern = functools.partial(_attn_kernel, tq=tq, blk=blk, heads=heads, lam_init=lam_init,
                             n_side=len(side_in))
    return pl.pallas_call(
        kern,
        grid=(n_seq, nh, nq),
        in_specs=[
            pl.BlockSpec((tq, width), lambda b, h, qi: (b * nq + qi, h)),
            pl.BlockSpec((seq_len, width), lambda b, h, qi: (b, h)),
            pl.BlockSpec((seq_len // blk, width, blk), lambda b, h, qi: (b, h, 0)),
            pl.BlockSpec((4, HEAD_DIM), lambda b, h, qi: (0, 0)),
            pl.BlockSpec((V_DIM, 1), lambda b, h, qi: (0, 0)),
        ] + side_in,
        out_specs=[pl.BlockSpec((tq, width), lambda b, h, qi: (b * nq + qi, h))] + side_out,
        out_shape=[jax.ShapeDtypeStruct((m, d), BF16)] + side_shapes,
        scratch_shapes=[pltpu.VMEM((heads, V_DIM, 2 * tq), F32), pltpu.VMEM((heads, 1, 2 * tq), F32),
                        pltpu.VMEM((heads, 1, 2 * tq), F32)],
        compiler_params=_cparams(("arbitrary", "arbitrary", "arbitrary")),
        name="prompt_diff_attn",
    )(q, kb, vt, lam_vecs, subln_g.reshape(V_DIM, 1), *side_args)


def _decode_kernel(pt_ref, qt_ref, kn_ref, vn_ref, lam_ref, sg_ref, *rest, n_heads, ts, lam_init):
    del pt_ref
    npg = PAGES_PER_STEP
    k_refs = rest[:npg]
    v_refs = rest[npg:2 * npg]
    o_ref, m_ref, l_ref, acc_ref = rest[2 * npg:2 * npg + 4]
    p_refs = rest[2 * npg + 4:]
    c = pl.program_id(1)
    nc = pl.num_programs(1)
    n_sub = 2 * n_heads
    cols = n_sub * ts
    keys = PAGE_SIZE

    hrow = lax.broadcasted_iota(jnp.int32, (n_sub, cols), 0)
    hlane = lax.broadcasted_iota(jnp.int32, (n_sub, cols), 1) // ts
    valid = hrow == hlane

    def to_col(a):
        row = jnp.sum(jnp.where(valid, a, 0.0), axis=0, keepdims=True)
        ri = lax.broadcasted_iota(jnp.int32, (cols, cols), 0)
        ci = lax.broadcasted_iota(jnp.int32, (cols, cols), 1)
        return jnp.sum(jnp.where(ri == ci, jnp.broadcast_to(row, (cols, cols)), 0.0), axis=1, keepdims=True)

    @pl.when(c == 0)
    def _():
        m_ref[...] = jnp.where(valid, -BIG, BIG)
        l_ref[...] = jnp.zeros(l_ref.shape, F32)
        acc_ref[...] = jnp.zeros(acc_ref.shape, F32)

    qt2 = qt_ref[...].astype(BF16)

    def flat_k(ref):
        return ref[...].reshape(keys * n_sub, HEAD_DIM).astype(BF16)

    def flat_v(ref):
        return ref[...].reshape(keys * n_heads, V_DIM).astype(BF16)

    def accumulate(state, r_list, v_list, scratch):
        m_old, l_old, acc = state
        m_new = m_old
        for r in r_list:
            m_new = jnp.maximum(m_new, jnp.max(r, axis=0))
        alpha = jnp.exp2(m_old - m_new)
        l_new = alpha * l_old
        o = None
        for r, v, p_ref in zip(r_list, v_list, scratch):
            p = jnp.exp2(r - m_new[None])
            l_new = l_new + jnp.sum(p, axis=0)
            p_ref[...] = p.reshape(keys * n_sub, cols)
            ps = (p_ref[pl.ds(0, keys * n_heads, stride=2), :]
                  + p_ref[pl.ds(1, keys * n_heads, stride=2), :]).astype(BF16)
            part = _dot_tn(ps, v)
            o = part if o is None else o + part
        return m_new, l_new, to_col(alpha) * acc + o

    def pair_scores(u):
        k2 = jnp.concatenate([flat_k(k_refs[u]), flat_k(k_refs[u + 1])], axis=1)
        r2 = _dot(k2, qt2)
        return [r2[:, :cols].reshape(keys, n_sub, cols), r2[:, cols:].reshape(keys, n_sub, cols)]

    state = (m_ref[...], l_ref[...], acc_ref[...])
    r_next = pair_scores(0)
    for u in range(0, npg, 2):
        r_pair = r_next
        if u + 2 < npg:
            r_next = pair_scores(u + 2)
        slot = u % 4
        state = accumulate(state, r_pair, [flat_v(v_refs[u]), flat_v(v_refs[u + 1])],
                           p_refs[slot:slot + 2])
    m_ref[...], l_ref[...], acc_ref[...] = state

    @pl.when(c == nc - 1)
    def _():
        r = _dot(kn_ref[...].astype(BF16), qt2[:HEAD_DIM, :cols]).reshape(keys, n_sub, cols)
        key = lax.broadcasted_iota(jnp.int32, r.shape, 0)
        qry = lax.broadcasted_iota(jnp.int32, r.shape, 2) % ts
        r = jnp.where(key <= qry, r, -BIG)
        _, l_fin, acc_fin = accumulate((m_ref[...], l_ref[...], acc_ref[...]), [r],
                                       [vn_ref[...].astype(BF16)], p_refs[:1])
        lam = _lambda_full(lam_ref, lam_init)
        on = acc_fin / to_col(l_fin)
        for hh in range(n_heads):
            o1 = on[(2 * hh) * ts:(2 * hh + 1) * ts]
            o2 = on[(2 * hh + 1) * ts:(2 * hh + 2) * ts]
            o = o1 - lam * o2
            o_ref[:, hh * V_DIM:(hh + 1) * V_DIM] = _rms(o, sg_ref[...]) * (1.0 - lam_init)


def _decode_call(page_table, q, k_new, v_new, lam_vecs, subln_g, cache_k, cache_v, *, a, lam_init):
    n_seq, n_pages = page_table.shape
    m, d = q.shape
    ts = m // n_seq
    n_heads = d // V_DIM
    n_sub = 2 * n_heads
    cols = n_sub * ts
    npg = PAGES_PER_STEP
    nc = n_pages // npg
    pad = PAGE_SIZE - ts
    qt = jnp.transpose(q.reshape(n_seq, ts, n_sub, HEAD_DIM), (0, 3, 2, 1)).reshape(n_seq, HEAD_DIM, cols)
    z = jnp.zeros_like(qt)
    qt2 = jnp.concatenate([jnp.concatenate([qt, z], axis=2), jnp.concatenate([z, qt], axis=2)], axis=1)
    kn = jnp.pad(k_new.reshape(n_seq, ts, n_sub, HEAD_DIM), ((0, 0), (0, pad), (0, 0), (0, 0)))
    vn = jnp.pad(v_new.reshape(n_seq, ts, n_heads, V_DIM), ((0, 0), (0, pad), (0, 0), (0, 0)))
    kn = kn.reshape(n_seq, PAGE_SIZE * n_sub, HEAD_DIM)
    vn = vn.reshape(n_seq, PAGE_SIZE * n_heads, V_DIM)

    def page_spec(shape, u):
        return pl.BlockSpec((None, None) + shape,
                            lambda b, c, pt: (a, pt[b, c * npg + u], 0, 0, 0))

    k_specs = [page_spec(cache_k.shape[2:], u) for u in range(npg)]
    v_specs = [page_spec(cache_v.shape[2:], u) for u in range(npg)]
    kern = functools.partial(_decode_kernel, n_heads=n_heads, ts=ts, lam_init=lam_init)
    grid_spec = pltpu.PrefetchScalarGridSpec(
        num_scalar_prefetch=1,
        grid=(n_seq, nc),
        in_specs=[
            pl.BlockSpec((None, 2 * HEAD_DIM, 2 * cols), lambda b, c, pt: (b, 0, 0)),
            pl.BlockSpec((None, PAGE_SIZE * n_sub, HEAD_DIM), lambda b, c, pt: (b, 0, 0)),
            pl.BlockSpec((None, PAGE_SIZE * n_heads, V_DIM), lambda b, c, pt: (b, 0, 0)),
            pl.BlockSpec((4, HEAD_DIM), lambda b, c, pt: (0, 0)),
            pl.BlockSpec((1, V_DIM), lambda b, c, pt: (0, 0)),
        ] + k_specs + v_specs,
        out_specs=pl.BlockSpec((ts, d), lambda b, c, pt: (b, 0)),
        scratch_shapes=[
            pltpu.VMEM((n_sub, cols), F32),
            pltpu.VMEM((n_sub, cols), F32),
            pltpu.VMEM((cols, V_DIM), F32),
        ] + [pltpu.VMEM((PAGE_SIZE * n_sub, cols), F32) for _ in range(4)],
    )
    return pl.pallas_call(
        kern,
        grid_spec=grid_spec,
        out_shape=jax.ShapeDtypeStruct((m, d), F32),
        compiler_params=_cparams(("arbitrary", "arbitrary")),
        name="sample_paged_diff_attn",
    )(page_table, qt2, kn, vn, lam_vecs, subln_g.reshape(1, V_DIM),
      *([cache_k] * npg), *([cache_v] * npg))


def _proj_kernel(a_ref, w_ref, x_ref, mod_ref, o_ref, *, per_row, tiles_per_seq):
    seq = pl.program_id(0) // tiles_per_seq
    mix = _dot(a_ref[...].astype(BF16), w_ref[...])
    gmod = _mod_rows(mod_ref, 2, seq, per_row, slice(None))
    o_ref[...] = x_ref[...] + gmod * mix


def _proj_call(act, w, x, mod, *, layer, w_index, tm, tn, per_row, tiles_per_seq):
    m, d = x.shape
    kdim = act.shape[1]
    rows = mod.shape[2]
    kern = functools.partial(_proj_kernel, per_row=per_row, tiles_per_seq=tiles_per_seq)
    return pl.pallas_call(
        kern,
        grid=(m // tm, d // tn),
        in_specs=[
            pl.BlockSpec((tm, kdim), lambda i, n: (i, 0)),
            pl.BlockSpec((None, kdim, tn), lambda i, n: (w_index, 0, n)),
            pl.BlockSpec((tm, tn), lambda i, n: (i, n)),
            pl.BlockSpec((None, 3, rows, tn), lambda i, n: (layer, 1, 0, n)),
        ],
        out_specs=pl.BlockSpec((tm, tn), lambda i, n: (i, n)),
        out_shape=jax.ShapeDtypeStruct((m, d), F32),
        compiler_params=_cparams(("arbitrary", "arbitrary")),
        name="gated_residual_proj",
    )(act, w, x, mod)


def _sgu_in_kernel(x_ref, mod_ref, g_ref, w_ref, *rest, per_row, tiles_per_seq, half, row_chunk,
                   n_side):
    side_in, rest = rest[:n_side], rest[n_side:]
    u_ref, v_ref, mean_ref, rstd_ref = rest[:4]
    side_out = rest[4:4 + n_side]
    h_ref, s1_ref, s2_ref = rest[4 + n_side:]
    _side_cast(side_in, side_out)
    i = pl.program_id(0)
    n = pl.program_id(1)
    nn = pl.num_programs(1)
    nu = nn // 2
    seq = i // tiles_per_seq
    w = w_ref[...]

    def step(first, is_v):
        for r in range(0, h_ref.shape[0], row_chunk):
            rows = slice(r, r + row_chunk)
            if first:
                h = _modulated(x_ref, mod_ref, g_ref, seq, per_row, rows)
                h_ref[rows, :] = h
            else:
                h = h_ref[rows, :]
            z = _dot(h, w)
            z = 0.5 * z * (1.0 + lax.erf(z * (2.0 ** -0.5)))
            if is_v:
                v_ref[rows, :] = z
                s1_ref[rows, :] += jnp.sum(z, axis=-1, keepdims=True)
                s2_ref[rows, :] += jnp.sum(z * z, axis=-1, keepdims=True)
            else:
                u_ref[rows, :] = z.astype(u_ref.dtype)

    @pl.when(n == 0)
    def _():
        s1_ref[...] = jnp.zeros(s1_ref.shape, F32)
        s2_ref[...] = jnp.zeros(s2_ref.shape, F32)
        step(True, False)

    @pl.when((n > 0) & (n < nu))
    def _():
        step(False, False)

    @pl.when(n >= nu)
    def _():
        step(False, True)

    @pl.when(n == nn - 1)
    def _():
        mean = s1_ref[...] * (1.0 / half)
        var = s2_ref[...] * (1.0 / half) - mean * mean
        mean_ref[...] = mean
        rstd_ref[...] = lax.rsqrt(var + EPS)


def _sgu_in_call(x, mod, norm_g, w_in, cast_weights, *, layer, g, tm, per_row, tiles_per_seq):
    m, d = x.shape
    ffn = w_in.shape[2]
    half = ffn // 2
    tn = 512
    nn = ffn // tn
    nu = nn // 2
    assert nu >= 2
    rows = mod.shape[2]
    g3 = norm_g.reshape(norm_g.shape[0], norm_g.shape[1], 1, d)
    side_in, side_args, side_out, side_shapes = _side_cast_plan(
        cast_weights, (m // tm) * nn, lambda i, n: i * nn + n)
    kern = functools.partial(_sgu_in_kernel, per_row=per_row, tiles_per_seq=tiles_per_seq, half=half,
                             row_chunk=min(tm, WIDE_ROW_CHUNK), n_side=len(side_in))
    return pl.pallas_call(
        kern,
        grid=(m // tm, nn),
        in_specs=[
            pl.BlockSpec((tm, d), lambda i, n: (i, 0)),
            pl.BlockSpec((None, 3, rows, d), lambda i, n: (layer, 1, 0, 0)),
            pl.BlockSpec((None, None, 1, d), lambda i, n: (layer, 1, 0, 0)),
            pl.BlockSpec((None, d, tn), lambda i, n: (g, 0, n)),
        ] + side_in,
        out_specs=[
            pl.BlockSpec((tm, tn), lambda i, n: (i, jnp.minimum(n, nu - 1))),
            pl.BlockSpec((tm, tn), lambda i, n: (i, jnp.maximum(n - nu, 0))),
            pl.BlockSpec((tm, 1), lambda i, n: (i, 0)),
            pl.BlockSpec((tm, 1), lambda i, n: (i, 0)),
        ] + side_out,
        out_shape=[
            jax.ShapeDtypeStruct((m, half), BF16),
            jax.ShapeDtypeStruct((m, half), F32),
            jax.ShapeDtypeStruct((m, 1), F32),
            jax.ShapeDtypeStruct((m, 1), F32),
        ] + side_shapes,
        scratch_shapes=[pltpu.VMEM((tm, d), BF16), pltpu.VMEM((tm, 1), F32), pltpu.VMEM((tm, 1), F32)],
        compiler_params=_cparams(("arbitrary", "arbitrary")),
        name="sgu_in",
    )(x, mod, g3, w_in, *side_args)


def _sgu_out_kernel(u_ref, v_ref, mean_ref, rstd_ref, lg_ref, lb_ref, ws_ref, bs_ref, w_ref, x_ref,
                    mod_ref, o_ref, vn_ref, *, per_row, tiles_per_seq, chunk_rows, row_chunk):
    i = pl.program_id(0)
    j = pl.program_id(1)
    nj = pl.num_programs(1)
    seq = i // tiles_per_seq
    cr = chunk_rows
    r = lax.broadcasted_iota(jnp.int32, (cr, cr), 0)
    c = lax.broadcasted_iota(jnp.int32, (cr, cr), 1)
    ws = jnp.where(c <= r, ws_ref[:cr, :cr], 0.0).astype(BF16)
    bs = bs_ref[:cr, :]
    w = w_ref[...]

    def step(first, last):
        for r0 in range(0, u_ref.shape[0], row_chunk):
            rows = slice(r0, r0 + row_chunk)
            gated = []
            for c0 in range(r0, r0 + row_chunk, cr):
                sl = slice(c0, c0 + cr)
                vn = (v_ref[sl, :] - mean_ref[sl, :]) * rstd_ref[sl, :] * lg_ref[...] + lb_ref[...]
                if vn_ref is not None:
                    vn_ref[sl, :] = vn
                s = _dot(ws, vn.astype(BF16)) + bs
                gated.append(u_ref[sl, :].astype(F32) * s)
            tot = _dot(jnp.concatenate(gated, axis=0).astype(BF16), w)
            if not first:
                tot = o_ref[rows, :] + tot
            if last:
                tot = x_ref[rows, :] + _mod_rows(mod_ref, 2, seq, per_row, rows) * tot
            o_ref[rows, :] = tot

    @pl.when(j == 0)
    def _():
        step(True, False)

    @pl.when((j > 0) & (j < nj - 1))
    def _():
        step(False, False)

    @pl.when(j == nj - 1)
    def _():
        step(False, True)


def _sgu_out_call(u, v, mean, rstd, ln_g, ln_b, w_s, b_s, w_out, x, mod, *, layer, g, tm, per_row,
                  tiles_per_seq, chunk_rows, want_vn):
    m, half = u.shape
    d = x.shape[1]
    gd = half // SGU_GROUPS
    rows = mod.shape[2]
    bs3 = b_s.reshape(b_s.shape[0], SGU_GROUPS, CHUNK, 1)
    lg3 = ln_g.reshape(ln_g.shape[0], 1, half)
    lb3 = ln_b.reshape(ln_b.shape[0], 1, half)
    params = dict(per_row=per_row, tiles_per_seq=tiles_per_seq, chunk_rows=chunk_rows,
                  row_chunk=min(tm, ROW_CHUNK))

    def kern(*refs):
        if want_vn:
            _sgu_out_kernel(*refs, **params)
        else:
            _sgu_out_kernel(*refs, None, **params)

    blk = pl.BlockSpec((tm, gd), lambda i, j: (i, j))
    out_specs = [pl.BlockSpec((tm, d), lambda i, j: (i, 0))]
    out_shape = [jax.ShapeDtypeStruct((m, d), F32)]
    if want_vn:
        out_specs.append(blk)
        out_shape.append(jax.ShapeDtypeStruct((m, half), F32))
    return pl.pallas_call(
        kern,
        grid=(m // tm, SGU_GROUPS),
        in_specs=[
            blk,
            blk,
            pl.BlockSpec((tm, 1), lambda i, j: (i, 0)),
            pl.BlockSpec((tm, 1), lambda i, j: (i, 0)),
            pl.BlockSpec((None, 1, gd), lambda i, j: (g, 0, j)),
            pl.BlockSpec((None, 1, gd), lambda i, j: (g, 0, j)),
            pl.BlockSpec((None, None, CHUNK, CHUNK), lambda i, j: (g, j, 0, 0)),
            pl.BlockSpec((None, None, CHUNK, 1), lambda i, j: (g, j, 0, 0)),
            pl.BlockSpec((None, gd, d), lambda i, j: (g, j, 0)),
            pl.BlockSpec((tm, d), lambda i, j: (i, 0)),
            pl.BlockSpec((None, 3, rows, d), lambda i, j: (layer, 1, 0, 0)),
        ],
        out_specs=out_specs,
        out_shape=out_shape,
        compiler_params=_cparams(("arbitrary", "arbitrary")),
        name="sgu_gate_out",
    )(u, v, mean, rstd, lg3, lb3, w_s, bs3, w_out, x, mod)


def _rotary_table(pos):
    half = HEAD_DIM // 2
    inv_freq = ROPE_THETA ** (-jnp.arange(half, dtype=F32) / half)
    ang = pos.astype(F32)[:, None] * inv_freq[None, :]
    cos, sin = jnp.cos(ang), jnp.sin(ang)
    return jnp.concatenate([cos, cos, -sin, sin], axis=-1)


def kernel(x_prompt, x_sample, c_prompt, c_sample, cache_k, cache_v, page_table, norm_g, w_ada, b_ada,
           ffn_w_in, ffn_w_out, attn_w_qkv, attn_w_o, attn_lambda, attn_subln_g, sgu_w_in, sgu_ln_g,
           sgu_ln_b, sgu_w_s, sgu_b_s, sgu_w_out, final_g):
    nb, t, d = x_prompt.shape
    db, ts, _ = x_sample.shape
    depth = w_ada.shape[0]
    n_pages = page_table.shape[1]
    past = n_pages * PAGE_SIZE
    mp, ms = nb * t, db * ts
    tm = min(ROW_TILE, t)
    tps = t // tm

    attn_w_qkv, attn_w_o, sgu_w_in, sgu_w_out = (
        w.astype(BF16) for w in (attn_w_qkv, attn_w_o, sgu_w_in, sgu_w_out))
    ffn_w = {(0, 0): (ffn_w_in[0, 0].astype(BF16), ffn_w_out[0, 0].astype(BF16))}

    def later_ffn_weights(i):
        sets = [(i, 1)] + ([(i + 1, 0)] if i + 1 < depth else [])
        return sets, [(w, li, wi) for li, wi in sets for w in (ffn_w_in, ffn_w_out)]

    def keep_ffn_weights(sets, casts):
        for k, key in enumerate(sets):
            ffn_w[key] = (casts[2 * k], casts[2 * k + 1])

    c_all = jnp.concatenate([c_prompt, c_sample, jnp.zeros((N_SEQ_PAD - nb - db, d), F32)], axis=0)
    mod = _ada_call(c_all, w_ada, b_ada)
    mod_s = jnp.repeat(mod[:, :, nb:nb + db], ts, axis=2)

    cs_p = _rotary_table(jnp.arange(t, dtype=jnp.int32))
    cs_s = jnp.tile(_rotary_table(past + jnp.arange(ts, dtype=jnp.int32)), (db, 1))

    xp = x_prompt.reshape(mp, d)
    xs = x_sample.reshape(ms, d)
    prm = dict(tm=tm, per_row=False, tiles_per_seq=tps)
    srm = dict(tm=ms, per_row=True, tiles_per_seq=1)
    kv_p = kv_s = None
    new_sgu_v = []

    for i in range(depth):
        last = i == depth - 1
        ffn = functools.partial(_ffn_call, norm_g=norm_g, final_g=final_g, layer=i)
        w_in0, w_out0 = ffn_w[(i, 0)]
        xp = ffn(xp, mod, w_in=w_in0, w_out=w_out0, sub=0, final_norm=False, **prm)
        xs = ffn(xs, mod_s, w_in=w_in0, w_out=w_out0, sub=0, final_norm=False, **srm)
        cast_sets, cast_weights = later_ffn_weights(i)
        if i % 2 == 0:
            a = i // 2
            lam_init = 0.8 - 0.6 * math.exp(-0.3 * i)
            qp, kp, vp, kpb, vpt = _qkv_call(xp, mod, norm_g, attn_w_qkv, cs_p, kv_p, layer=i, a=a,
                                             prompt=True, **prm)
            kv_p = (kp, vp)
            op, *casts = _attn_call(qp, kpb, vpt, attn_lambda[a], attn_subln_g[a], cast_weights,
                                    n_seq=nb, seq_len=t, lam_init=lam_init)
            keep_ffn_weights(cast_sets, casts)
            xp = _proj_call(op, attn_w_o, xp, mod, layer=i, w_index=a, tn=512, **prm)
            qs, ks_, vs_ = _qkv_call(xs, mod_s, norm_g, attn_w_qkv, cs_s, kv_s, layer=i, a=a,
                                     prompt=False, **srm)
            kv_s = (ks_, vs_)
            os_ = _decode_call(page_table, qs, ks_[a], vs_[a], attn_lambda[a], attn_subln_g[a],
                               cache_k, cache_v, a=a, lam_init=lam_init)
            xs = _proj_call(os_, attn_w_o, xs, mod_s, layer=i, w_index=a, tn=512, **srm)
        else:
            g = i // 2
            up, vp_, mean_p, rstd_p, *casts = _sgu_in_call(xp, mod, norm_g, sgu_w_in, cast_weights,
                                                           layer=i, g=g, **prm)
            keep_ffn_weights(cast_sets, casts)
            (xp,) = _sgu_out_call(up, vp_, mean_p, rstd_p, sgu_ln_g, sgu_ln_b, sgu_w_s, sgu_b_s,
                                  sgu_w_out, xp, mod, layer=i, g=g, chunk_rows=min(CHUNK, t),
                                  want_vn=False, **prm)
            us, vs2, mean_s, rstd_s = _sgu_in_call(xs, mod_s, norm_g, sgu_w_in, [], layer=i, g=g, **srm)
            xs, vn_s = _sgu_out_call(us, vs2, mean_s, rstd_s, sgu_ln_g, sgu_ln_b, sgu_w_s, sgu_b_s,
                                     sgu_w_out, xs, mod_s, layer=i, g=g, chunk_rows=ts,
                                     want_vn=True, **srm)
            new_sgu_v.append(vn_s)
        w_in1, w_out1 = ffn_w[(i, 1)]
        xp = ffn(xp, mod, w_in=w_in1, w_out=w_out1, sub=2, final_norm=last, **prm)
        xs = ffn(xs, mod_s, w_in=w_in1, w_out=w_out1, sub=2, final_norm=last, **srm)

    n_sub = 2 * (d // V_DIM)
    y_prompt = xp.reshape(nb, t, d)
    y_sample = xs.reshape(db, ts, d)
    k_prompt = kv_p[0].reshape(-1, nb, t, n_sub, HEAD_DIM)
    v_prompt = kv_p[1].reshape(-1, nb, t, n_sub // 2, V_DIM)
    k_sample = kv_s[0].reshape(-1, db, ts, n_sub, HEAD_DIM)
    v_sample = kv_s[1].reshape(-1, db, ts, n_sub // 2, V_DIM)
    sgu_v_sample = jnp.stack(new_sgu_v).reshape(len(new_sgu_v), db, ts, -1)
    return (y_prompt, y_sample, k_prompt, v_prompt, k_sample, v_sample, sgu_v_sample)
```

```python
import functools
import math

import jax
import jax.numpy as jnp
from jax import lax
from jax.experimental import pallas as pl
from jax.experimental.pallas import tpu as pltpu

F32 = jnp.float32
BF16 = jnp.bfloat16

EPS = 1e-6
ROPE_THETA = 10000.0
HEAD_DIM = 128
V_DIM = 2 * HEAD_DIM
Q_SCALE = HEAD_DIM ** -0.5 * math.log2(math.e)
CHUNK = 128
SGU_GROUPS = 8
PAGE_SIZE = 128
N_MOD = 9
N_SEQ_PAD = 16
BIG = 1e30

VMEM_LIMIT_BYTES = 62 * 1024 * 1024
ROW_TILE = 1024
ROW_CHUNK = 256
WIDE_ROW_CHUNK = 1024
ATT_BLOCK = 256
ATT_QUERY_BLOCK = 512
ATT_HEADS_PER_STEP = 2
PAGES_PER_STEP = 8


def _cparams(sem):
    return pltpu.CompilerParams(dimension_semantics=sem, vmem_limit_bytes=VMEM_LIMIT_BYTES)


def _dot(a, b):
    return jnp.dot(a, b, preferred_element_type=F32)


def _dot_nt(a, b):
    return lax.dot_general(a, b, (((1,), (1,)), ((), ())), preferred_element_type=F32)


def _dot_tn(a, b):
    return lax.dot_general(a, b, (((0,), (0,)), ((), ())), preferred_element_type=F32)


def _rms(x, g):
    return x * lax.rsqrt(jnp.mean(x * x, axis=-1, keepdims=True) + EPS) * g


def _mod_rows(mod_ref, k, seq, per_row, rows):
    if per_row:
        return mod_ref[k, rows, :]
    return mod_ref[k, pl.ds(seq, 1), :]


def _modulated(x_ref, mod_ref, g_ref, seq, per_row, rows):
    shift = _mod_rows(mod_ref, 0, seq, per_row, rows)
    scale = _mod_rows(mod_ref, 1, seq, per_row, rows)
    return (_rms(x_ref[rows, :], g_ref[...]) * (1.0 + scale) + shift).astype(BF16)


def _side_cast_plan(weights, n_steps, step_of):
    in_specs, args, out_specs, out_shapes = [], [], [], []
    for w4, li, wi in weights:
        n_rows, n_cols = w4.shape[2:]
        per = next(k for k in range(1, n_steps + 1)
                   if n_steps % k == 0 and n_rows % (n_steps // k) == 0
                   and (n_rows // (n_steps // k)) % 16 == 0)
        rb = n_rows // (n_steps // per)
        in_specs.append(pl.BlockSpec((None, None, rb, n_cols),
                                     lambda *g, li=li, wi=wi, per=per: (li, wi, step_of(*g) // per, 0)))
        out_specs.append(pl.BlockSpec((rb, n_cols), lambda *g, per=per: (step_of(*g) // per, 0)))
        out_shapes.append(jax.ShapeDtypeStruct((n_rows, n_cols), BF16))
        args.append(w4)
    return in_specs, args, out_specs, out_shapes


def _side_cast(in_refs, out_refs):
    for src, dst in zip(in_refs, out_refs):
        dst[...] = src[...].astype(BF16)


def _ada_kernel(c_ref, w_ref, b_ref, o_ref):
    c = c_ref[...]
    a = (c * jax.nn.sigmoid(c)).astype(BF16)
    o_ref[...] = _dot(a, w_ref[...].astype(BF16)) + b_ref[...]


def _ada_call(c_all, w_ada, b_ada):
    depth, d, _ = w_ada.shape
    tn = 1024
    nt = d // tn
    b4 = b_ada.reshape(depth, N_MOD, 1, d)
    return pl.pallas_call(
        _ada_kernel,
        grid=(depth, N_MOD * nt),
        in_specs=[
            pl.BlockSpec((N_SEQ_PAD, d), lambda l, n: (0, 0)),
            pl.BlockSpec((None, d, tn), lambda l, n: (l, 0, n)),
            pl.BlockSpec((None, None, 1, tn), lambda l, n: (l, n // nt, 0, n % nt)),
        ],
        out_specs=pl.BlockSpec((None, None, N_SEQ_PAD, tn), lambda l, n: (l, n // nt, 0, n % nt)),
        out_shape=jax.ShapeDtypeStruct((depth, N_MOD, N_SEQ_PAD, d), F32),
        compiler_params=_cparams(("arbitrary", "arbitrary")),
        name="adaln",
    )(c_all, w_ada, b4)


def _ffn_kernel(x_ref, mod_ref, g_ref, wg_ref, wu_ref, wo_ref, fg_ref, o_ref, h_ref, *,
                per_row, tiles_per_seq, final_norm, row_chunk):
    i = pl.program_id(0)
    j = pl.program_id(1)
    nj = pl.num_programs(1)
    seq = i // tiles_per_seq
    wg = wg_ref[...]
    wu = wu_ref[...]
    wo = wo_ref[...]

    def step(first, last):
        for r in range(0, h_ref.shape[0], row_chunk):
            rows = slice(r, r + row_chunk)
            if first:
                h = _modulated(x_ref, mod_ref, g_ref, seq, per_row, rows)
                h_ref[rows, :] = h
            else:
                h = h_ref[rows, :]
            gate = _dot(h, wg)
            up = _dot(h, wu)
            a = (gate * jax.nn.sigmoid(gate) * up).astype(BF16)
            tot = _dot(a, wo)
            if not first:
                tot = o_ref[rows, :] + tot
            if last:
                gmod = _mod_rows(mod_ref, 2, seq, per_row, rows)
                tot = x_ref[rows, :] + 0.5 * gmod * tot
                if final_norm:
                    tot = _rms(tot, fg_ref[...])
            o_ref[rows, :] = tot

    @pl.when(j == 0)
    def _():
        step(True, False)

    @pl.when((j > 0) & (j < nj - 1))
    def _():
        step(False, False)

    @pl.when(j == nj - 1)
    def _():
        step(False, True)


def _ffn_call(x, mod, norm_g, w_in, w_out, final_g, *, layer, sub, tm, per_row,
              tiles_per_seq, final_norm):
    m, d = x.shape
    d_ff = w_out.shape[0]
    tf = 512
    nj = d_ff // tf
    assert nj >= 2
    rows = mod.shape[2]
    g3 = norm_g.reshape(norm_g.shape[0], norm_g.shape[1], 1, d)
    kern = functools.partial(_ffn_kernel, per_row=per_row, tiles_per_seq=tiles_per_seq,
                             final_norm=final_norm, row_chunk=min(tm, WIDE_ROW_CHUNK))
    x_mode = dict(pipeline_mode=pl.Buffered(1)) if final_norm else {}
    return pl.pallas_call(
        kern,
        grid=(m // tm, nj),
        in_specs=[
            pl.BlockSpec((tm, d), lambda i, j: (i, 0), **x_mode),
            pl.BlockSpec((None, 3, rows, d), lambda i, j: (layer, sub, 0, 0)),
            pl.BlockSpec((None, None, 1, d), lambda i, j: (layer, sub, 0, 0)),
            pl.BlockSpec((d, tf), lambda i, j: (0, j)),
            pl.BlockSpec((d, tf), lambda i, j: (0, nj + j)),
            pl.BlockSpec((tf, d), lambda i, j: (j, 0)),
            pl.BlockSpec((1, d), lambda i, j: (0, 0)),
        ],
        out_specs=pl.BlockSpec((tm, d), lambda i, j: (i, 0)),
        out_shape=jax.ShapeDtypeStruct((m, d), F32),
        scratch_shapes=[pltpu.VMEM((tm, d), BF16)],
        compiler_params=_cparams(("arbitrary", "arbitrary")),
        name="ffn_half_step",
    )(x, mod, g3, w_in, w_in, w_out, final_g.reshape(1, d))


def _qkv_kernel(x_ref, mod_ref, g_ref, w_ref, cs_ref, *rest, per_row, tiles_per_seq, tn, prompt,
                aliased, row_chunk, n_side):
    if aliased:
        rest = rest[2:]
    side_in, rest = rest[:n_side], rest[n_side:]
    n_out = 5 if prompt else 3
    side_out = rest[n_out:n_out + n_side]
    _side_cast(side_in, side_out)
    if prompt:
        q_ref, k_ref, v_ref, kb_ref, vt_ref = rest[:n_out]
    else:
        q_ref, k_ref, v_ref = rest[:n_out]
    h_ref = rest[n_out + n_side]
    i = pl.program_id(0)
    p = pl.program_id(1)
    n = pl.program_id(2)
    seq = i // tiles_per_seq
    w = w_ref[...]

    def step(part, first):
        for ci, r in enumerate(range(0, h_ref.shape[0], row_chunk)):
            rows = slice(r, r + row_chunk)
            if first:
                h = _modulated(x_ref, mod_ref, g_ref, seq, per_row, rows)
                h_ref[rows, :] = h
            else:
                h = h_ref[rows, :]
            acc = _dot(h, w)
            if part == 2:
                v_ref[rows, :] = acc
                if prompt:
                    vt_ref[ci] = acc.T.astype(BF16)
                continue
            cos2 = cs_ref[rows, :HEAD_DIM]
            sin2 = cs_ref[rows, HEAD_DIM:]
            for hh in range(tn // HEAD_DIM):
                sl = slice(hh * HEAD_DIM, (hh + 1) * HEAD_DIM)
                xh = acc[:, sl]
                rot = xh * cos2 + pltpu.roll(xh, HEAD_DIM // 2, 1) * sin2
                if part == 0:
                    q_ref[rows, sl] = (rot * Q_SCALE).astype(q_ref.dtype)
                else:
                    k_ref[rows, sl] = rot
                    if prompt:
                        kb_ref[rows, sl] = rot.astype(BF16)

    @pl.when((p == 0) & (n == 0))
    def _():
        step(0, True)

    @pl.when((p == 0) & (n > 0))
    def _():
        step(0, False)

    @pl.when(p == 1)
    def _():
        step(1, False)

    @pl.when(p == 2)
    def _():
        step(2, False)


def _qkv_call(x, mod, norm_g, w_qkv, cs, kv_prev, cast_weights, *, layer, a, tm, per_row,
              tiles_per_seq, prompt):
    m, d = x.shape
    n_attn = w_qkv.shape[0]
    tn = 512
    nt = d // tn
    rows = mod.shape[2]
    g3 = norm_g.reshape(norm_g.shape[0], norm_g.shape[1], 1, d)
    n_cs_tiles = cs.shape[0] // tm
    row_chunk = min(tm, ATT_BLOCK)
    aliased = kv_prev is not None
    side_in, side_args, side_out, side_shapes = _side_cast_plan(
        cast_weights, (m // tm) * 3 * nt, lambda i, p, n: (i * 3 + p) * nt + n)
    kern = functools.partial(_qkv_kernel, per_row=per_row, tiles_per_seq=tiles_per_seq, tn=tn,
                             prompt=prompt, aliased=aliased, row_chunk=row_chunk, n_side=len(side_in))

    def part_col(part, p, n):
        return jnp.where(p < part, 0, jnp.where(p == part, n, nt - 1))

    def out_block(part):
        return pl.BlockSpec((tm, tn), lambda i, p, n: (i, part_col(part, p, n)))

    def stacked_block(part):
        return pl.BlockSpec((None, tm, tn), lambda i, p, n: (a, i, part_col(part, p, n)))

    in_specs = [
        pl.BlockSpec((tm, d), lambda i, p, n: (i, 0)),
        pl.BlockSpec((None, 3, rows, d), lambda i, p, n: (layer, 1, 0, 0)),
        pl.BlockSpec((None, None, 1, d), lambda i, p, n: (layer, 1, 0, 0)),
        pl.BlockSpec((None, d, tn), lambda i, p, n: (a, 0, p * nt + n)),
        pl.BlockSpec((tm, 2 * HEAD_DIM), lambda i, p, n: (i % n_cs_tiles, 0)),
    ]
    args = [x, mod, g3, w_qkv, cs]
    aliases = {}
    if aliased:
        in_specs += [pl.BlockSpec(memory_space=pl.ANY), pl.BlockSpec(memory_space=pl.ANY)]
        args += list(kv_prev)
        aliases = {5: 1, 6: 2}
    in_specs += side_in
    args += side_args
    out_specs = [out_block(0), stacked_block(1), stacked_block(2)]
    out_shape = [
        jax.ShapeDtypeStruct((m, d), BF16 if prompt else F32),
        jax.ShapeDtypeStruct((n_attn, m, d), F32),
        jax.ShapeDtypeStruct((n_attn, m, d), F32),
    ]
    if prompt:
        nb = tm // ATT_BLOCK
        out_specs += [out_block(1),
                      pl.BlockSpec((nb, tn, ATT_BLOCK), lambda i, p, n: (i, part_col(2, p, n), 0))]
        out_shape += [jax.ShapeDtypeStruct((m, d), BF16),
                      jax.ShapeDtypeStruct((m // ATT_BLOCK, d, ATT_BLOCK), BF16)]
    out_specs += side_out
    out_shape += side_shapes
    return pl.pallas_call(
        kern,
        grid=(m // tm, 3, nt),
        in_specs=in_specs,
        out_specs=out_specs,
        out_shape=out_shape,
        input_output_aliases=aliases,
        scratch_shapes=[pltpu.VMEM((tm, d), BF16)],
        compiler_params=_cparams(("arbitrary", "arbitrary", "arbitrary")),
        name="attn_qkv",
    )(*args)


def _lambda_full(lam_ref, lam_init):
    lv = lam_ref[...]
    d1 = jnp.sum(lv[0:1] * lv[1:2], axis=-1, keepdims=True)
    d2 = jnp.sum(lv[2:3] * lv[3:4], axis=-1, keepdims=True)
    return jnp.exp(d1) - jnp.exp(d2) + lam_init


def _attn_kernel(q_ref, k_ref, vt_ref, lam_ref, sg_ref, *rest, tq, blk, heads, lam_init, n_side):
    side_in, rest = rest[:n_side], rest[n_side:]
    o_ref, side_out = rest[0], rest[1:1 + n_side]
    acc_ref, m_ref, l_ref = rest[1 + n_side:]
    _side_cast(side_in, side_out)
    qi = pl.program_id(2)
    kq = tq // blk
    acc_ref[...] = jnp.zeros(acc_ref.shape, F32)
    m_ref[...] = jnp.full(m_ref.shape, -BIG, F32)
    l_ref[...] = jnp.zeros(l_ref.shape, F32)

    def scores(g, j):
        start = pl.multiple_of(j * tq, tq)
        c0 = g * V_DIM
        q1 = q_ref[:, c0:c0 + HEAD_DIM]
        q2 = q_ref[:, c0 + HEAD_DIM:c0 + V_DIM]
        k1 = k_ref[pl.ds(start, tq), c0:c0 + HEAD_DIM]
        k2 = k_ref[pl.ds(start, tq), c0 + HEAD_DIM:c0 + V_DIM]
        return jnp.concatenate([_dot_nt(k1, q1), _dot_nt(k2, q2)], axis=1)

    def update(g, s, j, diagonal):
        vt = jnp.concatenate([vt_ref[j * kq + d, g * V_DIM:(g + 1) * V_DIM, :] for d in range(kq)],
                             axis=1)
        if diagonal:
            key = lax.broadcasted_iota(jnp.int32, s.shape, 0)
            qry = lax.broadcasted_iota(jnp.int32, s.shape, 1) % tq
            s = jnp.where(key <= qry, s, -BIG)
        m_old = m_ref[g]
        m_new = jnp.maximum(m_old, jnp.max(s, axis=0, keepdims=True))
        p = jnp.exp2(s - m_new)
        alpha = jnp.exp2(m_old - m_new)
        m_ref[g] = m_new
        l_ref[g] = alpha * l_ref[g] + jnp.sum(p, axis=0, keepdims=True)
        acc_ref[g] = alpha * acc_ref[g] + _dot(vt, p.astype(BF16))

    def block(j, diagonal):
        ss = [scores(g, j) for g in range(heads)]
        for g in range(heads):
            update(g, ss[g], j, diagonal)

    def earlier_keys(j, carry):
        block(j, False)
        return carry

    lax.fori_loop(0, qi, earlier_keys, 0)
    block(qi, True)

    lam = _lambda_full(lam_ref, lam_init)
    for g in range(heads):
        on = acc_ref[g] / l_ref[g]
        o = on[:, :tq] - lam * on[:, tq:]
        o = o * lax.rsqrt(jnp.mean(o * o, axis=0, keepdims=True) + EPS) * sg_ref[...]
        o_ref[:, g * V_DIM:(g + 1) * V_DIM] = (o * (1.0 - lam_init)).T.astype(o_ref.dtype)


def _attn_call(q, kb, vt, lam_vecs, subln_g, cast_weights, *, n_seq, seq_len, lam_init,
               heads=ATT_HEADS_PER_STEP):
    m, d = q.shape
    n_heads = d // V_DIM
    width = heads * V_DIM
    blk = ATT_BLOCK
    tq = min(ATT_QUERY_BLOCK, seq_len)
    nq = seq_len // tq
    nh = n_heads // heads
    side_in, side_args, side_out, side_shapes = _side_cast_plan(
        cast_weights, n_seq * nh * nq, lambda b, h, qi: (b * nh + h) * nq + qi)
    kern = functools.partial(_attn_kernel, tq=tq, blk=blk, heads=heads, lam_init=lam_init,
                             n_side=len(side_in))
    return pl.pallas_call(
        kern,
        grid=(n_seq, nh, nq),
        in_specs=[
            pl.BlockSpec((tq, width), lambda b, h, qi: (b * nq + qi, h)),
            pl.BlockSpec((seq_len, width), lambda b, h, qi: (b, h)),
            pl.BlockSpec((seq_len // blk, width, blk), lambda b, h, qi: (b, h, 0)),
            pl.BlockSpec((4, HEAD_DIM), lambda b, h, qi: (0, 0)),
            pl.BlockSpec((V_DIM, 1), lambda b, h, qi: (0, 0)),
        ] + side_in,
        out_specs=[pl.BlockSpec((tq, width), lambda b, h, qi: (b * nq + qi, h))] + side_out,
        out_shape=[jax.ShapeDtypeStruct((m, d), BF16)] + side_shapes,
        scratch_shapes=[pltpu.VMEM((heads, V_DIM, 2 * tq), F32), pltpu.VMEM((heads, 1, 2 * tq), F32),
                        pltpu.VMEM((heads, 1, 2 * tq), F32)],
        compiler_params=_cparams(("arbitrary", "arbitrary", "arbitrary")),
        name="prompt_diff_attn",
    )(q, kb, vt, lam_vecs, subln_g.reshape(V_DIM, 1), *side_args)


def _decode_kernel(pt_ref, qt_ref, kn_ref, vn_ref, lam_ref, sg_ref, *rest, n_heads, ts, lam_init):
    del pt_ref
    npg = PAGES_PER_STEP
    k_refs = rest[:npg]
    v_refs = rest[npg:2 * npg]
    o_ref, m_ref, l_ref, acc_ref = rest[2 * npg:2 * npg + 4]
    p_refs = rest[2 * npg + 4:]
    c = pl.program_id(1)
    nc = pl.num_programs(1)
    n_sub = 2 * n_heads
    cols = n_sub * ts
    keys = PAGE_SIZE

    hrow = lax.broadcasted_iota(jnp.int32, (n_sub, cols), 0)
    hlane = lax.broadcasted_iota(jnp.int32, (n_sub, cols), 1) // ts
    valid = hrow == hlane

    def to_col(a):
        row = jnp.sum(jnp.where(valid, a, 0.0), axis=0, keepdims=True)
        ri = lax.broadcasted_iota(jnp.int32, (cols, cols), 0)
        ci = lax.broadcasted_iota(jnp.int32, (cols, cols), 1)
        return jnp.sum(jnp.where(ri == ci, jnp.broadcast_to(row, (cols, cols)), 0.0), axis=1, keepdims=True)

    @pl.when(c == 0)
    def _():
        m_ref[...] = jnp.where(valid, -BIG, BIG)
        l_ref[...] = jnp.zeros(l_ref.shape, F32)
        acc_ref[...] = jnp.zeros(acc_ref.shape, F32)

    qt2 = qt_ref[...].astype(BF16)

    def flat_k(ref):
        return ref[...].reshape(keys * n_sub, HEAD_DIM).astype(BF16)

    def flat_v(ref):
        return ref[...].reshape(keys * n_heads, V_DIM).astype(BF16)

    def accumulate(state, r_list, v_list, scratch):
        m_old, l_old, acc = state
        m_new = m_old
        for r in r_list:
            m_new = jnp.maximum(m_new, jnp.max(r, axis=0))
        alpha = jnp.exp2(m_old - m_new)
        l_new = alpha * l_old
        o = None
        for r, v, p_ref in zip(r_list, v_list, scratch):
            p = jnp.exp2(r - m_new[None])
            l_new = l_new + jnp.sum(p, axis=0)
            p_ref[...] = p.reshape(keys * n_sub, cols)
            ps = (p_ref[pl.ds(0, keys * n_heads, stride=2), :]
                  + p_ref[pl.ds(1, keys * n_heads, stride=2), :]).astype(BF16)
            part = _dot_tn(ps, v)
            o = part if o is None else o + part
        return m_new, l_new, to_col(alpha) * acc + o

    def pair_scores(u):
        k2 = jnp.concatenate([flat_k(k_refs[u]), flat_k(k_refs[u + 1])], axis=1)
        r2 = _dot(k2, qt2)
        return [r2[:, :cols].reshape(keys, n_sub, cols), r2[:, cols:].reshape(keys, n_sub, cols)]

    state = (m_ref[...], l_ref[...], acc_ref[...])
    r_next = pair_scores(0)
    for u in range(0, npg, 2):
        r_pair = r_next
        if u + 2 < npg:
            r_next = pair_scores(u + 2)
        slot = u % 4
        state = accumulate(state, r_pair, [flat_v(v_refs[u]), flat_v(v_refs[u + 1])],
                           p_refs[slot:slot + 2])
    m_ref[...], l_ref[...], acc_ref[...] = state

    @pl.when(c == nc - 1)
    def _():
        r = _dot(kn_ref[...].astype(BF16), qt2[:HEAD_DIM, :cols]).reshape(keys, n_sub, cols)
        key = lax.broadcasted_iota(jnp.int32, r.shape, 0)
        qry = lax.broadcasted_iota(jnp.int32, r.shape, 2) % ts
        r = jnp.where(key <= qry, r, -BIG)
        _, l_fin, acc_fin = accumulate((m_ref[...], l_ref[...], acc_ref[...]), [r],
                                       [vn_ref[...].astype(BF16)], p_refs[:1])
        lam = _lambda_full(lam_ref, lam_init)
        on = acc_fin / to_col(l_fin)
        for hh in range(n_heads):
            o1 = on[(2 * hh) * ts:(2 * hh + 1) * ts]
            o2 = on[(2 * hh + 1) * ts:(2 * hh + 2) * ts]
            o = o1 - lam * o2
            o_ref[:, hh * V_DIM:(hh + 1) * V_DIM] = _rms(o, sg_ref[...]) * (1.0 - lam_init)


def _decode_call(page_table, q, k_new, v_new, lam_vecs, subln_g, cache_k, cache_v, *, a, lam_init):
    n_seq, n_pages = page_table.shape
    m, d = q.shape
    ts = m // n_seq
    n_heads = d // V_DIM
    n_sub = 2 * n_heads
    cols = n_sub * ts
    npg = PAGES_PER_STEP
    nc = n_pages // npg
    pad = PAGE_SIZE - ts
    qt = jnp.transpose(q.reshape(n_seq, ts, n_sub, HEAD_DIM), (0, 3, 2, 1)).reshape(n_seq, HEAD_DIM, cols)
    z = jnp.zeros_like(qt)
    qt2 = jnp.concatenate([jnp.concatenate([qt, z], axis=2), jnp.concatenate([z, qt], axis=2)], axis=1)
    kn = jnp.pad(k_new.reshape(n_seq, ts, n_sub, HEAD_DIM), ((0, 0), (0, pad), (0, 0), (0, 0)))
    vn = jnp.pad(v_new.reshape(n_seq, ts, n_heads, V_DIM), ((0, 0), (0, pad), (0, 0), (0, 0)))
    kn = kn.reshape(n_seq, PAGE_SIZE * n_sub, HEAD_DIM)
    vn = vn.reshape(n_seq, PAGE_SIZE * n_heads, V_DIM)

    def page_spec(shape, u):
        return pl.BlockSpec((None, None) + shape,
                            lambda b, c, pt: (a, pt[b, c * npg + u], 0, 0, 0))

    k_specs = [page_spec(cache_k.shape[2:], u) for u in range(npg)]
    v_specs = [page_spec(cache_v.shape[2:], u) for u in range(npg)]
    kern = functools.partial(_decode_kernel, n_heads=n_heads, ts=ts, lam_init=lam_init)
    grid_spec = pltpu.PrefetchScalarGridSpec(
        num_scalar_prefetch=1,
        grid=(n_seq, nc),
        in_specs=[
            pl.BlockSpec((None, 2 * HEAD_DIM, 2 * cols), lambda b, c, pt: (b, 0, 0)),
            pl.BlockSpec((None, PAGE_SIZE * n_sub, HEAD_DIM), lambda b, c, pt: (b, 0, 0)),
            pl.BlockSpec((None, PAGE_SIZE * n_heads, V_DIM), lambda b, c, pt: (b, 0, 0)),
            pl.BlockSpec((4, HEAD_DIM), lambda b, c, pt: (0, 0)),
            pl.BlockSpec((1, V_DIM), lambda b, c, pt: (0, 0)),
        ] + k_specs + v_specs,
        out_specs=pl.BlockSpec((ts, d), lambda b, c, pt: (b, 0)),
        scratch_shapes=[
            pltpu.VMEM((n_sub, cols), F32),
            pltpu.VMEM((n_sub, cols), F32),
            pltpu.VMEM((cols, V_DIM), F32),
        ] + [pltpu.VMEM((PAGE_SIZE * n_sub, cols), F32) for _ in range(4)],
    )
    return pl.pallas_call(
        kern,
        grid_spec=grid_spec,
        out_shape=jax.ShapeDtypeStruct((m, d), F32),
        compiler_params=_cparams(("arbitrary", "arbitrary")),
        name="sample_paged_diff_attn",
    )(page_table, qt2, kn, vn, lam_vecs, subln_g.reshape(1, V_DIM),
      *([cache_k] * npg), *([cache_v] * npg))


def _proj_kernel(a_ref, w_ref, x_ref, mod_ref, o_ref, *, per_row, tiles_per_seq):
    seq = pl.program_id(0) // tiles_per_seq
    mix = _dot(a_ref[...].astype(BF16), w_ref[...])
    gmod = _mod_rows(mod_ref, 2, seq, per_row, slice(None))
    o_ref[...] = x_ref[...] + gmod * mix


def _proj_call(act, w, x, mod, *, layer, w_index, tm, tn, per_row, tiles_per_seq):
    m, d = x.shape
    kdim = act.shape[1]
    rows = mod.shape[2]
    kern = functools.partial(_proj_kernel, per_row=per_row, tiles_per_seq=tiles_per_seq)
    return pl.pallas_call(
        kern,
        grid=(m // tm, d // tn),
        in_specs=[
            pl.BlockSpec((tm, kdim), lambda i, n: (i, 0)),
            pl.BlockSpec((None, kdim, tn), lambda i, n: (w_index, 0, n)),
            pl.BlockSpec((tm, tn), lambda i, n: (i, n)),
            pl.BlockSpec((None, 3, rows, tn), lambda i, n: (layer, 1, 0, n)),
        ],
        out_specs=pl.BlockSpec((tm, tn), lambda i, n: (i, n)),
        out_shape=jax.ShapeDtypeStruct((m, d), F32),
        compiler_params=_cparams(("arbitrary", "arbitrary")),
        name="gated_residual_proj",
    )(act, w, x, mod)


def _sgu_in_kernel(x_ref, mod_ref, g_ref, w_ref, *rest, per_row, tiles_per_seq, half, row_chunk,
                   n_side):
    side_in, rest = rest[:n_side], rest[n_side:]
    u_ref, v_ref, mean_ref, rstd_ref = rest[:4]
    side_out = rest[4:4 + n_side]
    h_ref, s1_ref, s2_ref = rest[4 + n_side:]
    _side_cast(side_in, side_out)
    i = pl.program_id(0)
    n = pl.program_id(1)
    nn = pl.num_programs(1)
    nu = nn // 2
    seq = i // tiles_per_seq
    w = w_ref[...]

    def step(first, is_v):
        for r in range(0, h_ref.shape[0], row_chunk):
            rows = slice(r, r + row_chunk)
            if first:
                h = _modulated(x_ref, mod_ref, g_ref, seq, per_row, rows)
                h_ref[rows, :] = h
            else:
                h = h_ref[rows, :]
            z = _dot(h, w)
            z = 0.5 * z * (1.0 + lax.erf(z * (2.0 ** -0.5)))
            if is_v:
                v_ref[rows, :] = z
                s1_ref[rows, :] += jnp.sum(z, axis=-1, keepdims=True)
                s2_ref[rows, :] += jnp.sum(z * z, axis=-1, keepdims=True)
            else:
                u_ref[rows, :] = z.astype(u_ref.dtype)

    @pl.when(n == 0)
    def _():
        s1_ref[...] = jnp.zeros(s1_ref.shape, F32)
        s2_ref[...] = jnp.zeros(s2_ref.shape, F32)
        step(True, False)

    @pl.when((n > 0) & (n < nu))
    def _():
        step(False, False)

    @pl.when(n >= nu)
    def _():
        step(False, True)

    @pl.when(n == nn - 1)
    def _():
        mean = s1_ref[...] * (1.0 / half)
        var = s2_ref[...] * (1.0 / half) - mean * mean
        mean_ref[...] = mean
        rstd_ref[...] = lax.rsqrt(var + EPS)


def _sgu_in_call(x, mod, norm_g, w_in, cast_weights, *, layer, g, tm, per_row, tiles_per_seq):
    m, d = x.shape
    ffn = w_in.shape[2]
    half = ffn // 2
    tn = 512
    nn = ffn // tn
    nu = nn // 2
    assert nu >= 2
    rows = mod.shape[2]
    g3 = norm_g.reshape(norm_g.shape[0], norm_g.shape[1], 1, d)
    side_in, side_args, side_out, side_shapes = _side_cast_plan(
        cast_weights, (m // tm) * nn, lambda i, n: i * nn + n)
    kern = functools.partial(_sgu_in_kernel, per_row=per_row, tiles_per_seq=tiles_per_seq, half=half,
                             row_chunk=min(tm, WIDE_ROW_CHUNK), n_side=len(side_in))
    return pl.pallas_call(
        kern,
        grid=(m // tm, nn),
        in_specs=[
            pl.BlockSpec((tm, d), lambda i, n: (i, 0)),
            pl.BlockSpec((None, 3, rows, d), lambda i, n: (layer, 1, 0, 0)),
            pl.BlockSpec((None, None, 1, d), lambda i, n: (layer, 1, 0, 0)),
            pl.BlockSpec((None, d, tn), lambda i, n: (g, 0, n)),
        ] + side_in,
        out_specs=[
            pl.BlockSpec((tm, tn), lambda i, n: (i, jnp.minimum(n, nu - 1))),
            pl.BlockSpec((tm, tn), lambda i, n: (i, jnp.maximum(n - nu, 0))),
            pl.BlockSpec((tm, 1), lambda i, n: (i, 0)),
            pl.BlockSpec((tm, 1), lambda i, n: (i, 0)),
        ] + side_out,
        out_shape=[
            jax.ShapeDtypeStruct((m, half), BF16),
            jax.ShapeDtypeStruct((m, half), F32),
            jax.ShapeDtypeStruct((m, 1), F32),
            jax.ShapeDtypeStruct((m, 1), F32),
        ] + side_shapes,
        scratch_shapes=[pltpu.VMEM((tm, d), BF16), pltpu.VMEM((tm, 1), F32), pltpu.VMEM((tm, 1), F32)],
        compiler_params=_cparams(("arbitrary", "arbitrary")),
        name="sgu_in",
    )(x, mod, g3, w_in, *side_args)


def _sgu_out_kernel(u_ref, v_ref, mean_ref, rstd_ref, lg_ref, lb_ref, ws_ref, bs_ref, w_ref, x_ref,
                    mod_ref, o_ref, vn_ref, *, per_row, tiles_per_seq, chunk_rows, row_chunk):
    i = pl.program_id(0)
    j = pl.program_id(1)
    nj = pl.num_programs(1)
    seq = i // tiles_per_seq
    cr = chunk_rows
    r = lax.broadcasted_iota(jnp.int32, (cr, cr), 0)
    c = lax.broadcasted_iota(jnp.int32, (cr, cr), 1)
    ws = jnp.where(c <= r, ws_ref[:cr, :cr], 0.0).astype(BF16)
    bs = bs_ref[:cr, :]
    w = w_ref[...]

    def step(first, last):
        for r0 in range(0, u_ref.shape[0], row_chunk):
            rows = slice(r0, r0 + row_chunk)
            gated = []
            for c0 in range(r0, r0 + row_chunk, cr):
                sl = slice(c0, c0 + cr)
                vn = (v_ref[sl, :] - mean_ref[sl, :]) * rstd_ref[sl, :] * lg_ref[...] + lb_ref[...]
                if vn_ref is not None:
                    vn_ref[sl, :] = vn
                s = _dot(ws, vn.astype(BF16)) + bs
                gated.append(u_ref[sl, :].astype(F32) * s)
            tot = _dot(jnp.concatenate(gated, axis=0).astype(BF16), w)
            if not first:
                tot = o_ref[rows, :] + tot
            if last:
                tot = x_ref[rows, :] + _mod_rows(mod_ref, 2, seq, per_row, rows) * tot
            o_ref[rows, :] = tot

    @pl.when(j == 0)
    def _():
        step(True, False)

    @pl.when((j > 0) & (j < nj - 1))
    def _():
        step(False, False)

    @pl.when(j == nj - 1)
    def _():
        step(False, True)


def _sgu_out_call(u, v, mean, rstd, ln_g, ln_b, w_s, b_s, w_out, x, mod, *, layer, g, tm, per_row,
                  tiles_per_seq, chunk_rows, want_vn):
    m, half = u.shape
    d = x.shape[1]
    gd = half // SGU_GROUPS
    rows = mod.shape[2]
    bs3 = b_s.reshape(b_s.shape[0], SGU_GROUPS, CHUNK, 1)
    lg3 = ln_g.reshape(ln_g.shape[0], 1, half)
    lb3 = ln_b.reshape(ln_b.shape[0], 1, half)
    params = dict(per_row=per_row, tiles_per_seq=tiles_per_seq, chunk_rows=chunk_rows,
                  row_chunk=min(tm, ROW_CHUNK))

    def kern(*refs):
        if want_vn:
            _sgu_out_kernel(*refs, **params)
        else:
            _sgu_out_kernel(*refs, None, **params)

    blk = pl.BlockSpec((tm, gd), lambda i, j: (i, j))
    out_specs = [pl.BlockSpec((tm, d), lambda i, j: (i, 0))]
    out_shape = [jax.ShapeDtypeStruct((m, d), F32)]
    if want_vn:
        out_specs.append(blk)
        out_shape.append(jax.ShapeDtypeStruct((m, half), F32))
    return pl.pallas_call(
        kern,
        grid=(m // tm, SGU_GROUPS),
        in_specs=[
            blk,
            blk,
            pl.BlockSpec((tm, 1), lambda i, j: (i, 0)),
            pl.BlockSpec((tm, 1), lambda i, j: (i, 0)),
            pl.BlockSpec((None, 1, gd), lambda i, j: (g, 0, j)),
            pl.BlockSpec((None, 1, gd), lambda i, j: (g, 0, j)),
            pl.BlockSpec((None, None, CHUNK, CHUNK), lambda i, j: (g, j, 0, 0)),
            pl.BlockSpec((None, None, CHUNK, 1), lambda i, j: (g, j, 0, 0)),
            pl.BlockSpec((None, gd, d), lambda i, j: (g, j, 0)),
            pl.BlockSpec((tm, d), lambda i, j: (i, 0)),
            pl.BlockSpec((None, 3, rows, d), lambda i, j: (layer, 1, 0, 0)),
        ],
        out_specs=out_specs,
        out_shape=out_shape,
        compiler_params=_cparams(("arbitrary", "arbitrary")),
        name="sgu_gate_out",
    )(u, v, mean, rstd, lg3, lb3, w_s, bs3, w_out, x, mod)


def _rotary_table(pos):
    half = HEAD_DIM // 2
    inv_freq = ROPE_THETA ** (-jnp.arange(half, dtype=F32) / half)
    ang = pos.astype(F32)[:, None] * inv_freq[None, :]
    cos, sin = jnp.cos(ang), jnp.sin(ang)
    return jnp.concatenate([cos, cos, -sin, sin], axis=-1)


def kernel(x_prompt, x_sample, c_prompt, c_sample, cache_k, cache_v, page_table, norm_g, w_ada, b_ada,
           ffn_w_in, ffn_w_out, attn_w_qkv, attn_w_o, attn_lambda, attn_subln_g, sgu_w_in, sgu_ln_g,
           sgu_ln_b, sgu_w_s, sgu_b_s, sgu_w_out, final_g):
    nb, t, d = x_prompt.shape
    db, ts, _ = x_sample.shape
    depth = w_ada.shape[0]
    n_pages = page_table.shape[1]
    past = n_pages * PAGE_SIZE
    mp, ms = nb * t, db * ts
    tm = min(ROW_TILE, t)
    tps = t // tm

    attn_w_qkv, attn_w_o, sgu_w_in, sgu_w_out = (
        w.astype(BF16) for w in (attn_w_qkv, attn_w_o, sgu_w_in, sgu_w_out))
    ffn_w = {(0, 0): (ffn_w_in[0, 0].astype(BF16), ffn_w_out[0, 0].astype(BF16))}

    def cast_plan(sets):
        sets = [s for s in sets if s[0] < depth]
        return sets, [(w, li, wi) for li, wi in sets for w in (ffn_w_in, ffn_w_out)]

    def keep_ffn_weights(sets, casts):
        for k, key in enumerate(sets):
            ffn_w[key] = (casts[2 * k], casts[2 * k + 1])

    c_all = jnp.concatenate([c_prompt, c_sample, jnp.zeros((N_SEQ_PAD - nb - db, d), F32)], axis=0)
    mod = _ada_call(c_all, w_ada, b_ada)
    mod_s = jnp.repeat(mod[:, :, nb:nb + db], ts, axis=2)

    cs_p = _rotary_table(jnp.arange(t, dtype=jnp.int32))
    cs_s = jnp.tile(_rotary_table(past + jnp.arange(ts, dtype=jnp.int32)), (db, 1))

    xp = x_prompt.reshape(mp, d)
    xs = x_sample.reshape(ms, d)
    prm = dict(tm=tm, per_row=False, tiles_per_seq=tps)
    srm = dict(tm=ms, per_row=True, tiles_per_seq=1)
    kv_p = kv_s = None
    new_sgu_v = []

    for i in range(depth):
        last = i == depth - 1
        ffn = functools.partial(_ffn_call, norm_g=norm_g, final_g=final_g, layer=i)
        w_in0, w_out0 = ffn_w[(i, 0)]
        xp = ffn(xp, mod, w_in=w_in0, w_out=w_out0, sub=0, final_norm=False, **prm)
        xs = ffn(xs, mod_s, w_in=w_in0, w_out=w_out0, sub=0, final_norm=False, **srm)
        if i % 2 == 0:
            a = i // 2
            lam_init = 0.8 - 0.6 * math.exp(-0.3 * i)
            sets, weights = cast_plan([(i, 1)])
            qp, kp, vp, kpb, vpt, *casts = _qkv_call(xp, mod, norm_g, attn_w_qkv, cs_p, kv_p, weights,
                                                     layer=i, a=a, prompt=True, **prm)
            keep_ffn_weights(sets, casts)
            kv_p = (kp, vp)
            sets, weights = cast_plan([(i + 1, 0), (i + 1, 1), (i + 2, 0)])
            op, *casts = _attn_call(qp, kpb, vpt, attn_lambda[a], attn_subln_g[a], weights,
                                    n_seq=nb, seq_len=t, lam_init=lam_init)
            keep_ffn_weights(sets, casts)
            xp = _proj_call(op, attn_w_o, xp, mod, layer=i, w_index=a, tn=512, **prm)
            qs, ks_, vs_ = _qkv_call(xs, mod_s, norm_g, attn_w_qkv, cs_s, kv_s, [], layer=i, a=a,
                                     prompt=False, **srm)
            kv_s = (ks_, vs_)
            os_ = _decode_call(page_table, qs, ks_[a], vs_[a], attn_lambda[a], attn_subln_g[a],
                               cache_k, cache_v, a=a, lam_init=lam_init)
            xs = _proj_call(os_, attn_w_o, xs, mod_s, layer=i, w_index=a, tn=512, **srm)
        else:
            g = i // 2
            up, vp_, mean_p, rstd_p = _sgu_in_call(xp, mod, norm_g, sgu_w_in, [], layer=i, g=g, **prm)
            (xp,) = _sgu_out_call(up, vp_, mean_p, rstd_p, sgu_ln_g, sgu_ln_b, sgu_w_s, sgu_b_s,
                                  sgu_w_out, xp, mod, layer=i, g=g, chunk_rows=min(CHUNK, t),
                                  want_vn=False, **prm)
            us, vs2, mean_s, rstd_s = _sgu_in_call(xs, mod_s, norm_g, sgu_w_in, [], layer=i, g=g, **srm)
            xs, vn_s = _sgu_out_call(us, vs2, mean_s, rstd_s, sgu_ln_g, sgu_ln_b, sgu_w_s, sgu_b_s,
                                     sgu_w_out, xs, mod_s, layer=i, g=g, chunk_rows=ts,
                                     want_vn=True, **srm)
            new_sgu_v.append(vn_s)
        w_in1, w_out1 = ffn_w[(i, 1)]
        xp = ffn(xp, mod, w_in=w_in1, w_out=w_out1, sub=2, final_norm=last, **prm)
        xs = ffn(xs, mod_s, w_in=w_in1, w_out=w_out1, sub=2, final_norm=last, **srm)

    n_sub = 2 * (d // V_DIM)
    y_prompt = xp.reshape(nb, t, d)
    y_sample = xs.reshape(db, ts, d)
    k_prompt = kv_p[0].reshape(-1, nb, t, n_sub, HEAD_DIM)
    v_prompt = kv_p[1].reshape(-1, nb, t, n_sub // 2, V_DIM)
    k_sample = kv_s[0].reshape(-1, db, ts, n_sub, HEAD_DIM)
    v_sample = kv_s[1].reshape(-1, db, ts, n_sub // 2, V_DIM)
    sgu_v_sample = jnp.stack(new_sgu_v).reshape(len(new_sgu_v), db, ts, -1)
    return (y_prompt, y_sample, k_prompt, v_prompt, k_sample, v_sample, sgu_v_sample)
```

```python
import functools
import math

import jax
import jax.numpy as jnp
from jax import lax
from jax.experimental import pallas as pl
from jax.experimental.pallas import tpu as pltpu

F32 = jnp.float32
BF16 = jnp.bfloat16

EPS = 1e-6
ROPE_THETA = 10000.0
HEAD_DIM = 128
V_DIM = 2 * HEAD_DIM
Q_SCALE = HEAD_DIM ** -0.5 * math.log2(math.e)
CHUNK = 128
SGU_GROUPS = 8
PAGE_SIZE = 128
N_MOD = 9
N_SEQ_PAD = 16
BIG = 1e30

VMEM_LIMIT_BYTES = 62 * 1024 * 1024
ROW_TILE = 1024
ROW_CHUNK = 256
WIDE_ROW_CHUNK = 1024
ATT_BLOCK = 256
ATT_QUERY_BLOCK = 512
ATT_HEADS_PER_STEP = 2
PAGES_PER_STEP = 8


def _cparams(sem):
    return pltpu.CompilerParams(dimension_semantics=sem, vmem_limit_bytes=VMEM_LIMIT_BYTES)


def _dot(a, b):
    return jnp.dot(a, b, preferred_element_type=F32)


def _dot_nt(a, b):
    return lax.dot_general(a, b, (((1,), (1,)), ((), ())), preferred_element_type=F32)


def _dot_tn(a, b):
    return lax.dot_general(a, b, (((0,), (0,)), ((), ())), preferred_element_type=F32)


def _rms(x, g):
    return x * lax.rsqrt(jnp.mean(x * x, axis=-1, keepdims=True) + EPS) * g


def _mod_rows(mod_ref, k, seq, per_row, rows):
    if per_row:
        return mod_ref[k, rows, :]
    return mod_ref[k, pl.ds(seq, 1), :]


def _modulated(x_ref, mod_ref, g_ref, seq, per_row, rows):
    shift = _mod_rows(mod_ref, 0, seq, per_row, rows)
    scale = _mod_rows(mod_ref, 1, seq, per_row, rows)
    return (_rms(x_ref[rows, :], g_ref[...]) * (1.0 + scale) + shift).astype(BF16)


def _side_cast_plan(weights, n_steps, step_of):
    in_specs, args, out_specs, out_shapes = [], [], [], []
    for w4, li, wi in weights:
        n_rows, n_cols = w4.shape[2:]
        per = next(k for k in range(1, n_steps + 1)
                   if n_steps % k == 0 and n_rows % (n_steps // k) == 0
                   and (n_rows // (n_steps // k)) % 16 == 0)
        rb = n_rows // (n_steps // per)
        in_specs.append(pl.BlockSpec((None, None, rb, n_cols),
                                     lambda *g, li=li, wi=wi, per=per: (li, wi, step_of(*g) // per, 0)))
        out_specs.append(pl.BlockSpec((rb, n_cols), lambda *g, per=per: (step_of(*g) // per, 0)))
        out_shapes.append(jax.ShapeDtypeStruct((n_rows, n_cols), BF16))
        args.append(w4)
    return in_specs, args, out_specs, out_shapes


def _side_cast(in_refs, out_refs):
    for src, dst in zip(in_refs, out_refs):
        dst[...] = src[...].astype(BF16)


def _ada_kernel(c_ref, w_ref, b_ref, o_ref):
    c = c_ref[...]
    a = (c * jax.nn.sigmoid(c)).astype(BF16)
    o_ref[...] = _dot(a, w_ref[...].astype(BF16)) + b_ref[...]


def _ada_call(c_all, w_ada, b_ada):
    depth, d, _ = w_ada.shape
    tn = 1024
    nt = d // tn
    b4 = b_ada.reshape(depth, N_MOD, 1, d)
    return pl.pallas_call(
        _ada_kernel,
        grid=(depth, N_MOD * nt),
        in_specs=[
            pl.BlockSpec((N_SEQ_PAD, d), lambda l, n: (0, 0)),
            pl.BlockSpec((None, d, tn), lambda l, n: (l, 0, n)),
            pl.BlockSpec((None, None, 1, tn), lambda l, n: (l, n // nt, 0, n % nt)),
        ],
        out_specs=pl.BlockSpec((None, None, N_SEQ_PAD, tn), lambda l, n: (l, n // nt, 0, n % nt)),
        out_shape=jax.ShapeDtypeStruct((depth, N_MOD, N_SEQ_PAD, d), F32),
        compiler_params=_cparams(("arbitrary", "arbitrary")),
        name="adaln",
    )(c_all, w_ada, b4)


def _ffn_kernel(x_ref, mod_ref, g_ref, wg_ref, wu_ref, wo_ref, fg_ref, o_ref, h_ref, *,
                per_row, tiles_per_seq, final_norm, row_chunk):
    i = pl.program_id(0)
    j = pl.program_id(1)
    nj = pl.num_programs(1)
    seq = i // tiles_per_seq
    wg = wg_ref[...]
    wu = wu_ref[...]
    wo = wo_ref[...]

    def step(first, last):
        for r in range(0, h_ref.shape[0], row_chunk):
            rows = slice(r, r + row_chunk)
            if first:
                h = _modulated(x_ref, mod_ref, g_ref, seq, per_row, rows)
                h_ref[rows, :] = h
            else:
                h = h_ref[rows, :]
            gate = _dot(h, wg)
            up = _dot(h, wu)
            a = (gate * jax.nn.sigmoid(gate) * up).astype(BF16)
            tot = _dot(a, wo)
            if not first:
                tot = o_ref[rows, :] + tot
            if last:
                gmod = _mod_rows(mod_ref, 2, seq, per_row, rows)
                tot = x_ref[rows, :] + 0.5 * gmod * tot
                if final_norm:
                    tot = _rms(tot, fg_ref[...])
            o_ref[rows, :] = tot

    @pl.when(j == 0)
    def _():
        step(True, False)

    @pl.when((j > 0) & (j < nj - 1))
    def _():
        step(False, False)

    @pl.when(j == nj - 1)
    def _():
        step(False, True)


def _ffn_call(x, mod, norm_g, w_in, w_out, final_g, *, layer, sub, tm, per_row,
              tiles_per_seq, final_norm):
    m, d = x.shape
    d_ff = w_out.shape[0]
    tf = 512
    nj = d_ff // tf
    assert nj >= 2
    rows = mod.shape[2]
    g3 = norm_g.reshape(norm_g.shape[0], norm_g.shape[1], 1, d)
    kern = functools.partial(_ffn_kernel, per_row=per_row, tiles_per_seq=tiles_per_seq,
                             final_norm=final_norm, row_chunk=min(tm, WIDE_ROW_CHUNK))
    x_mode = dict(pipeline_mode=pl.Buffered(1)) if final_norm else {}
    return pl.pallas_call(
        kern,
        grid=(m // tm, nj),
        in_specs=[
            pl.BlockSpec((tm, d), lambda i, j: (i, 0), **x_mode),
            pl.BlockSpec((None, 3, rows, d), lambda i, j: (layer, sub, 0, 0)),
            pl.BlockSpec((None, None, 1, d), lambda i, j: (layer, sub, 0, 0)),
            pl.BlockSpec((d, tf), lambda i, j: (0, j)),
            pl.BlockSpec((d, tf), lambda i, j: (0, nj + j)),
            pl.BlockSpec((tf, d), lambda i, j: (j, 0)),
            pl.BlockSpec((1, d), lambda i, j: (0, 0)),
        ],
        out_specs=pl.BlockSpec((tm, d), lambda i, j: (i, 0)),
        out_shape=jax.ShapeDtypeStruct((m, d), F32),
        scratch_shapes=[pltpu.VMEM((tm, d), BF16)],
        compiler_params=_cparams(("arbitrary", "arbitrary")),
        name="ffn_half_step",
    )(x, mod, g3, w_in, w_in, w_out, final_g.reshape(1, d))


def _qkv_kernel(x_ref, mod_ref, g_ref, w_ref, cs_ref, *rest, per_row, tiles_per_seq, tn, prompt,
                aliased, row_chunk, n_side):
    if aliased:
        rest = rest[2:]
    side_in, rest = rest[:n_side], rest[n_side:]
    n_out = 5 if prompt else 3
    side_out = rest[n_out:n_out + n_side]
    _side_cast(side_in, side_out)
    if prompt:
        q_ref, k_ref, v_ref, kb_ref, vt_ref = rest[:n_out]
    else:
        q_ref, k_ref, v_ref = rest[:n_out]
    h_ref = rest[n_out + n_side]
    i = pl.program_id(0)
    p = pl.program_id(1)
    n = pl.program_id(2)
    seq = i // tiles_per_seq
    w = w_ref[...]

    def step(part, first):
        for ci, r in enumerate(range(0, h_ref.shape[0], row_chunk)):
            rows = slice(r, r + row_chunk)
            if first:
                h = _modulated(x_ref, mod_ref, g_ref, seq, per_row, rows)
                h_ref[rows, :] = h
            else:
                h = h_ref[rows, :]
            acc = _dot(h, w)
            if part == 2:
                v_ref[rows, :] = acc
                if prompt:
                    vt_ref[ci] = acc.T.astype(BF16)
                continue
            cos2 = cs_ref[rows, :HEAD_DIM]
            sin2 = cs_ref[rows, HEAD_DIM:]
            for hh in range(tn // HEAD_DIM):
                sl = slice(hh * HEAD_DIM, (hh + 1) * HEAD_DIM)
                xh = acc[:, sl]
                rot = xh * cos2 + pltpu.roll(xh, HEAD_DIM // 2, 1) * sin2
                if part == 0:
                    q_ref[rows, sl] = (rot * Q_SCALE).astype(q_ref.dtype)
                else:
                    k_ref[rows, sl] = rot
                    if prompt:
                        kb_ref[rows, sl] = rot.astype(BF16)

    @pl.when((p == 0) & (n == 0))
    def _():
        step(0, True)

    @pl.when((p == 0) & (n > 0))
    def _():
        step(0, False)

    @pl.when(p == 1)
    def _():
        step(1, False)

    @pl.when(p == 2)
    def _():
        step(2, False)


def _qkv_call(x, mod, norm_g, w_qkv, cs, kv_prev, cast_weights, *, layer, a, tm, per_row,
              tiles_per_seq, prompt):
    m, d = x.shape
    n_attn = w_qkv.shape[0]
    tn = 512
    nt = d // tn
    rows = mod.shape[2]
    g3 = norm_g.reshape(norm_g.shape[0], norm_g.shape[1], 1, d)
    n_cs_tiles = cs.shape[0] // tm
    row_chunk = min(tm, ATT_BLOCK)
    aliased = kv_prev is not None
    side_in, side_args, side_out, side_shapes = _side_cast_plan(
        cast_weights, (m // tm) * 3 * nt, lambda i, p, n: (i * 3 + p) * nt + n)
    kern = functools.partial(_qkv_kernel, per_row=per_row, tiles_per_seq=tiles_per_seq, tn=tn,
                             prompt=prompt, aliased=aliased, row_chunk=row_chunk, n_side=len(side_in))

    def part_col(part, p, n):
        return jnp.where(p < part, 0, jnp.where(p == part, n, nt - 1))

    def out_block(part):
        return pl.BlockSpec((tm, tn), lambda i, p, n: (i, part_col(part, p, n)))

    def stacked_block(part):
        return pl.BlockSpec((None, tm, tn), lambda i, p, n: (a, i, part_col(part, p, n)))

    in_specs = [
        pl.BlockSpec((tm, d), lambda i, p, n: (i, 0)),
        pl.BlockSpec((None, 3, rows, d), lambda i, p, n: (layer, 1, 0, 0)),
        pl.BlockSpec((None, None, 1, d), lambda i, p, n: (layer, 1, 0, 0)),
        pl.BlockSpec((None, d, tn), lambda i, p, n: (a, 0, p * nt + n)),
        pl.BlockSpec((tm, 2 * HEAD_DIM), lambda i, p, n: (i % n_cs_tiles, 0)),
    ]
    args = [x, mod, g3, w_qkv, cs]
    aliases = {}
    if aliased:
        in_specs += [pl.BlockSpec(memory_space=pl.ANY), pl.BlockSpec(memory_space=pl.ANY)]
        args += list(kv_prev)
        aliases = {5: 1, 6: 2}
    in_specs += side_in
    args += side_args
    out_specs = [out_block(0), stacked_block(1), stacked_block(2)]
    out_shape = [
        jax.ShapeDtypeStruct((m, d), BF16 if prompt else F32),
        jax.ShapeDtypeStruct((n_attn, m, d), F32),
        jax.ShapeDtypeStruct((n_attn, m, d), F32),
    ]
    if prompt:
        nb = tm // ATT_BLOCK
        out_specs += [out_block(1),
                      pl.BlockSpec((nb, tn, ATT_BLOCK), lambda i, p, n: (i, part_col(2, p, n), 0))]
        out_shape += [jax.ShapeDtypeStruct((m, d), BF16),
                      jax.ShapeDtypeStruct((m // ATT_BLOCK, d, ATT_BLOCK), BF16)]
    out_specs += side_out
    out_shape += side_shapes
    return pl.pallas_call(
        kern,
        grid=(m // tm, 3, nt),
        in_specs=in_specs,
        out_specs=out_specs,
        out_shape=out_shape,
        input_output_aliases=aliases,
        scratch_shapes=[pltpu.VMEM((tm, d), BF16)],
        compiler_params=_cparams(("arbitrary", "arbitrary", "arbitrary")),
        name="attn_qkv",
    )(*args)


def _lambda_full(lam_ref, lam_init):
    lv = lam_ref[...]
    d1 = jnp.sum(lv[0:1] * lv[1:2], axis=-1, keepdims=True)
    d2 = jnp.sum(lv[2:3] * lv[3:4], axis=-1, keepdims=True)
    return jnp.exp(d1) - jnp.exp(d2) + lam_init


def _attn_kernel(q_ref, k_ref, vt_ref, lam_ref, sg_ref, *rest, tq, blk, heads, lam_init, n_side):
    side_in, rest = rest[:n_side], rest[n_side:]
    o_ref, side_out = rest[0], rest[1:1 + n_side]
    acc_ref, m_ref, l_ref = rest[1 + n_side:]
    _side_cast(side_in, side_out)
    qi = pl.program_id(2)
    kq = tq // blk
    acc_ref[...] = jnp.zeros(acc_ref.shape, F32)
    m_ref[...] = jnp.full(m_ref.shape, -BIG, F32)
    l_ref[...] = jnp.zeros(l_ref.shape, F32)

    def scores(g, j):
        start = pl.multiple_of(j * tq, tq)
        c0 = g * V_DIM
        q1 = q_ref[:, c0:c0 + HEAD_DIM]
        q2 = q_ref[:, c0 + HEAD_DIM:c0 + V_DIM]
        k1 = k_ref[pl.ds(start, tq), c0:c0 + HEAD_DIM]
        k2 = k_ref[pl.ds(start, tq), c0 + HEAD_DIM:c0 + V_DIM]
        return jnp.concatenate([_dot_nt(k1, q1), _dot_nt(k2, q2)], axis=1)

    def update(g, s, j, diagonal):
        vt = jnp.concatenate([vt_ref[j * kq + d, g * V_DIM:(g + 1) * V_DIM, :] for d in range(kq)],
                             axis=1)
        if diagonal:
            key = lax.broadcasted_iota(jnp.int32, s.shape, 0)
            qry = lax.broadcasted_iota(jnp.int32, s.shape, 1) % tq
            s = jnp.where(key <= qry, s, -BIG)
        m_old = m_ref[g]
        m_new = jnp.maximum(m_old, jnp.max(s, axis=0, keepdims=True))
        p = jnp.exp2(s - m_new)
        alpha = jnp.exp2(m_old - m_new)
        m_ref[g] = m_new
        l_ref[g] = alpha * l_ref[g] + jnp.sum(p, axis=0, keepdims=True)
        acc_ref[g] = alpha * acc_ref[g] + _dot(vt, p.astype(BF16))

    def block(j, diagonal):
        ss = [scores(g, j) for g in range(heads)]
        for g in range(heads):
            update(g, ss[g], j, diagonal)

    def earlier_keys(j, carry):
        block(j, False)
        return carry

    lax.fori_loop(0, qi, earlier_keys, 0)
    block(qi, True)

    lam = _lambda_full(lam_ref, lam_init)
    for g in range(heads):
        on = acc_ref[g] / l_ref[g]
        o = on[:, :tq] - lam * on[:, tq:]
        o = o * lax.rsqrt(jnp.mean(o * o, axis=0, keepdims=True) + EPS) * sg_ref[...]
        o_ref[:, g * V_DIM:(g + 1) * V_DIM] = (o * (1.0 - lam_init)).T.astype(o_ref.dtype)


def _attn_call(q, kb, vt, lam_vecs, subln_g, cast_weights, *, n_seq, seq_len, lam_init,
               heads=ATT_HEADS_PER_STEP):
    m, d = q.shape
    n_heads = d // V_DIM
    width = heads * V_DIM
    blk = ATT_BLOCK
    tq = min(ATT_QUERY_BLOCK, seq_len)
    nq = seq_len // tq
    nh = n_heads // heads
    side_in, side_args, side_out, side_shapes = _side_cast_plan(
        cast_weights, n_seq * nh * nq, lambda b, h, qi: (b * nh + h) * nq + qi)
    kern = functools.partial(_attn_kernel, tq=tq, blk=blk, heads=heads, lam_init=lam_init,
                             n_side=len(side_in))
    return pl.pallas_call(
        kern,
        grid=(n_seq, nh, nq),
        in_specs=[
            pl.BlockSpec((tq, width), lambda b, h, qi: (b * nq + qi, h)),
            pl.BlockSpec((seq_len, width), lambda b, h, qi: (b, h)),
            pl.BlockSpec((seq_len // blk, width, blk), lambda b, h, qi: (b, h, 0)),
            pl.BlockSpec((4, HEAD_DIM), lambda b, h, qi: (0, 0)),
            pl.BlockSpec((V_DIM, 1), lambda b, h, qi: (0, 0)),
        ] + side_in,
        out_specs=[pl.BlockSpec((tq, width), lambda b, h, qi: (b * nq + qi, h))] + side_out,
        out_shape=[jax.ShapeDtypeStruct((m, d), BF16)] + side_shapes,
        scratch_shapes=[pltpu.VMEM((heads, V_DIM, 2 * tq), F32), pltpu.VMEM((heads, 1, 2 * tq), F32),
                        pltpu.VMEM((heads, 1, 2 * tq), F32)],
        compiler_params=_cparams(("arbitrary", "arbitrary", "arbitrary")),
        name="prompt_diff_attn",
    )(q, kb, vt, lam_vecs, subln_g.reshape(V_DIM, 1), *side_args)


def _decode_kernel(pt_ref, qt_ref, kn_ref, vn_ref, lam_ref, sg_ref, *rest, n_heads, ts, lam_init):
    del pt_ref
    npg = PAGES_PER_STEP
    k_refs = rest[:npg]
    v_refs = rest[npg:2 * npg]
    o_ref, m_ref, l_ref, acc_ref = rest[2 * npg:2 * npg + 4]
    p_refs = rest[2 * npg + 4:]
    c = pl.program_id(1)
    nc = pl.num_programs(1)
    n_sub = 2 * n_heads
    cols = n_sub * ts
    keys = PAGE_SIZE

    hrow = lax.broadcasted_iota(jnp.int32, (n_sub, cols), 0)
    hlane = lax.broadcasted_iota(jnp.int32, (n_sub, cols), 1) // ts
    valid = hrow == hlane

    def to_col(a):
        row = jnp.sum(jnp.where(valid, a, 0.0), axis=0, keepdims=True)
        ri = lax.broadcasted_iota(jnp.int32, (cols, cols), 0)
        ci = lax.broadcasted_iota(jnp.int32, (cols, cols), 1)
        return jnp.sum(jnp.where(ri == ci, jnp.broadcast_to(row, (cols, cols)), 0.0), axis=1, keepdims=True)

    @pl.when(c == 0)
    def _():
        m_ref[...] = jnp.where(valid, -BIG, BIG)
        l_ref[...] = jnp.zeros(l_ref.shape, F32)
        acc_ref[...] = jnp.zeros(acc_ref.shape, F32)

    qt2 = qt_ref[...].astype(BF16)

    def flat_k(ref):
        return ref[...].reshape(keys * n_sub, HEAD_DIM).astype(BF16)

    def flat_v(ref):
        return ref[...].reshape(keys * n_heads, V_DIM).astype(BF16)

    def accumulate(state, r_list, v_list, scratch):
        m_old, l_old, acc = state
        m_new = m_old
        for r in r_list:
            m_new = jnp.maximum(m_new, jnp.max(r, axis=0))
        alpha = jnp.exp2(m_old - m_new)
        l_new = alpha * l_old
        o = None
        for r, v, p_ref in zip(r_list, v_list, scratch):
            p = jnp.exp2(r - m_new[None])
            l_new = l_new + jnp.sum(p, axis=0)
            p_ref[...] = p.reshape(keys * n_sub, cols)
            ps = (p_ref[pl.ds(0, keys * n_heads, stride=2), :]
                  + p_ref[pl.ds(1, keys * n_heads, stride=2), :]).astype(BF16)
            part = _dot_tn(ps, v)
            o = part if o is None else o + part
        return m_new, l_new, to_col(alpha) * acc + o

    def pair_scores(u):
        k2 = jnp.concatenate([flat_k(k_refs[u]), flat_k(k_refs[u + 1])], axis=1)
        r2 = _dot(k2, qt2)
        return [r2[:, :cols].reshape(keys, n_sub, cols), r2[:, cols:].reshape(keys, n_sub, cols)]

    state = (m_ref[...], l_ref[...], acc_ref[...])
    r_next = pair_scores(0)
    for u in range(0, npg, 2):
        r_pair = r_next
        if u + 2 < npg:
            r_next = pair_scores(u + 2)
        slot = u % 4
        state = accumulate(state, r_pair, [flat_v(v_refs[u]), flat_v(v_refs[u + 1])],
                           p_refs[slot:slot + 2])
    m_ref[...], l_ref[...], acc_ref[...] = state

    @pl.when(c == nc - 1)
    def _():
        r = _dot(kn_ref[...].astype(BF16), qt2[:HEAD_DIM, :cols]).reshape(keys, n_sub, cols)
        key = lax.broadcasted_iota(jnp.int32, r.shape, 0)
        qry = lax.broadcasted_iota(jnp.int32, r.shape, 2) % ts
        r = jnp.where(key <= qry, r, -BIG)
        _, l_fin, acc_fin = accumulate((m_ref[...], l_ref[...], acc_ref[...]), [r],
                                       [vn_ref[...].astype(BF16)], p_refs[:1])
        lam = _lambda_full(lam_ref, lam_init)
        on = acc_fin / to_col(l_fin)
        for hh in range(n_heads):
            o1 = on[(2 * hh) * ts:(2 * hh + 1) * ts]
            o2 = on[(2 * hh + 1) * ts:(2 * hh + 2) * ts]
            o = o1 - lam * o2
            o_ref[:, hh * V_DIM:(hh + 1) * V_DIM] = _rms(o, sg_ref[...]) * (1.0 - lam_init)


def _decode_call(page_table, q, k_new, v_new, lam_vecs, subln_g, cache_k, cache_v, *, a, lam_init):
    n_seq, n_pages = page_table.shape
    m, d = q.shape
    ts = m // n_seq
    n_heads = d // V_DIM
    n_sub = 2 * n_heads
    cols = n_sub * ts
    npg = PAGES_PER_STEP
    nc = n_pages // npg
    pad = PAGE_SIZE - ts
    qt = jnp.transpose(q.reshape(n_seq, ts, n_sub, HEAD_DIM), (0, 3, 2, 1)).reshape(n_seq, HEAD_DIM, cols)
    z = jnp.zeros_like(qt)
    qt2 = jnp.concatenate([jnp.concatenate([qt, z], axis=2), jnp.concatenate([z, qt], axis=2)], axis=1)
    kn = jnp.pad(k_new.reshape(n_seq, ts, n_sub, HEAD_DIM), ((0, 0), (0, pad), (0, 0), (0, 0)))
    vn = jnp.pad(v_new.reshape(n_seq, ts, n_heads, V_DIM), ((0, 0), (0, pad), (0, 0), (0, 0)))
    kn = kn.reshape(n_seq, PAGE_SIZE * n_sub, HEAD_DIM)
    vn = vn.reshape(n_seq, PAGE_SIZE * n_heads, V_DIM)

    def page_spec(shape, u):
        return pl.BlockSpec((None, None) + shape,
                            lambda b, c, pt: (a, pt[b, c * npg + u], 0, 0, 0))

    k_specs = [page_spec(cache_k.shape[2:], u) for u in range(npg)]
    v_specs = [page_spec(cache_v.shape[2:], u) for u in range(npg)]
    kern = functools.partial(_decode_kernel, n_heads=n_heads, ts=ts, lam_init=lam_init)
    grid_spec = pltpu.PrefetchScalarGridSpec(
        num_scalar_prefetch=1,
        grid=(n_seq, nc),
        in_specs=[
            pl.BlockSpec((None, 2 * HEAD_DIM, 2 * cols), lambda b, c, pt: (b, 0, 0)),
            pl.BlockSpec((None, PAGE_SIZE * n_sub, HEAD_DIM), lambda b, c, pt: (b, 0, 0)),
            pl.BlockSpec((None, PAGE_SIZE * n_heads, V_DIM), lambda b, c, pt: (b, 0, 0)),
            pl.BlockSpec((4, HEAD_DIM), lambda b, c, pt: (0, 0)),
            pl.BlockSpec((1, V_DIM), lambda b, c, pt: (0, 0)),
        ] + k_specs + v_specs,
        out_specs=pl.BlockSpec((ts, d), lambda b, c, pt: (b, 0)),
        scratch_shapes=[
            pltpu.VMEM((n_sub, cols), F32),
            pltpu.VMEM((n_sub, cols), F32),
            pltpu.VMEM((cols, V_DIM), F32),
        ] + [pltpu.VMEM((PAGE_SIZE * n_sub, cols), F32) for _ in range(4)],
    )
    return pl.pallas_call(
        kern,
        grid_spec=grid_spec,
        out_shape=jax.ShapeDtypeStruct((m, d), F32),
        compiler_params=_cparams(("arbitrary", "arbitrary")),
        name="sample_paged_diff_attn",
    )(page_table, qt2, kn, vn, lam_vecs, subln_g.reshape(1, V_DIM),
      *([cache_k] * npg), *([cache_v] * npg))


def _proj_kernel(a_ref, w_ref, x_ref, mod_ref, o_ref, *, per_row, tiles_per_seq):
    seq = pl.program_id(0) // tiles_per_seq
    mix = _dot(a_ref[...].astype(BF16), w_ref[...])
    gmod = _mod_rows(mod_ref, 2, seq, per_row, slice(None))
    o_ref[...] = x_ref[...] + gmod * mix


def _proj_call(act, w, x, mod, *, layer, w_index, tm, tn, per_row, tiles_per_seq):
    m, d = x.shape
    kdim = act.shape[1]
    rows = mod.shape[2]
    kern = functools.partial(_proj_kernel, per_row=per_row, tiles_per_seq=tiles_per_seq)
    return pl.pallas_call(
        kern,
        grid=(m // tm, d // tn),
        in_specs=[
            pl.BlockSpec((tm, kdim), lambda i, n: (i, 0)),
            pl.BlockSpec((None, kdim, tn), lambda i, n: (w_index, 0, n)),
            pl.BlockSpec((tm, tn), lambda i, n: (i, n)),
            pl.BlockSpec((None, 3, rows, tn), lambda i, n: (layer, 1, 0, n)),
        ],
        out_specs=pl.BlockSpec((tm, tn), lambda i, n: (i, n)),
        out_shape=jax.ShapeDtypeStruct((m, d), F32),
        compiler_params=_cparams(("arbitrary", "arbitrary")),
        name="gated_residual_proj",
    )(act, w, x, mod)


def _sgu_in_kernel(x_ref, mod_ref, g_ref, w_ref, *rest, per_row, tiles_per_seq, half, row_chunk,
                   n_side):
    side_in, rest = rest[:n_side], rest[n_side:]
    u_ref, v_ref, mean_ref, rstd_ref = rest[:4]
    side_out = rest[4:4 + n_side]
    h_ref, s1_ref, s2_ref = rest[4 + n_side:]
    _side_cast(side_in, side_out)
    i = pl.program_id(0)
    n = pl.program_id(1)
    nn = pl.num_programs(1)
    nu = nn // 2
    seq = i // tiles_per_seq
    w = w_ref[...]

    def step(first, is_v):
        for r in range(0, h_ref.shape[0], row_chunk):
            rows = slice(r, r + row_chunk)
            if first:
                h = _modulated(x_ref, mod_ref, g_ref, seq, per_row, rows)
                h_ref[rows, :] = h
            else:
                h = h_ref[rows, :]
            z = _dot(h, w)
            z = 0.5 * z * (1.0 + lax.erf(z * (2.0 ** -0.5)))
            if is_v:
                v_ref[rows, :] = z
                s1_ref[rows, :] += jnp.sum(z, axis=-1, keepdims=True)
                s2_ref[rows, :] += jnp.sum(z * z, axis=-1, keepdims=True)
            else:
                u_ref[rows, :] = z.astype(u_ref.dtype)

    @pl.when(n == 0)
    def _():
        s1_ref[...] = jnp.zeros(s1_ref.shape, F32)
        s2_ref[...] = jnp.zeros(s2_ref.shape, F32)
        step(True, False)

    @pl.when((n > 0) & (n < nu))
    def _():
        step(False, False)

    @pl.when(n >= nu)
    def _():
        step(False, True)

    @pl.when(n == nn - 1)
    def _():
        mean = s1_ref[...] * (1.0 / half)
        var = s2_ref[...] * (1.0 / half) - mean * mean
        mean_ref[...] = mean
        rstd_ref[...] = lax.rsqrt(var + EPS)


def _sgu_in_call(x, mod, norm_g, w_in, cast_weights, *, layer, g, tm, per_row, tiles_per_seq):
    m, d = x.shape
    ffn = w_in.shape[2]
    half = ffn // 2
    tn = 512
    nn = ffn // tn
    nu = nn // 2
    assert nu >= 2
    rows = mod.shape[2]
    g3 = norm_g.reshape(norm_g.shape[0], norm_g.shape[1], 1, d)
    side_in, side_args, side_out, side_shapes = _side_cast_plan(
        cast_weights, (m // tm) * nn, lambda i, n: i * nn + n)
    kern = functools.partial(_sgu_in_kernel, per_row=per_row, tiles_per_seq=tiles_per_seq, half=half,
                             row_chunk=min(tm, WIDE_ROW_CHUNK), n_side=len(side_in))
    return pl.pallas_call(
        kern,
        grid=(m // tm, nn),
        in_specs=[
            pl.BlockSpec((tm, d), lambda i, n: (i, 0)),
            pl.BlockSpec((None, 3, rows, d), lambda i, n: (layer, 1, 0, 0)),
            pl.BlockSpec((None, None, 1, d), lambda i, n: (layer, 1, 0, 0)),
            pl.BlockSpec((None, d, tn), lambda i, n: (g, 0, n)),
        ] + side_in,
        out_specs=[
            pl.BlockSpec((tm, tn), lambda i, n: (i, jnp.minimum(n, nu - 1))),
            pl.BlockSpec((tm, tn), lambda i, n: (i, jnp.maximum(n - nu, 0))),
            pl.BlockSpec((tm, 1), lambda i, n: (i, 0)),
            pl.BlockSpec((tm, 1), lambda i, n: (i, 0)),
        ] + side_out,
        out_shape=[
            jax.ShapeDtypeStruct((m, half), BF16),
            jax.ShapeDtypeStruct((m, half), F32),
            jax.ShapeDtypeStruct((m, 1), F32),
            jax.ShapeDtypeStruct((m, 1), F32),
        ] + side_shapes,
        scratch_shapes=[pltpu.VMEM((tm, d), BF16), pltpu.VMEM((tm, 1), F32), pltpu.VMEM((tm, 1), F32)],
        compiler_params=_cparams(("arbitrary", "arbitrary")),
        name="sgu_in",
    )(x, mod, g3, w_in, *side_args)


def _sgu_out_kernel(u_ref, v_ref, mean_ref, rstd_ref, lg_ref, lb_ref, ws_ref, bs_ref, w_ref, x_ref,
                    mod_ref, o_ref, vn_ref, *, per_row, tiles_per_seq, chunk_rows, row_chunk):
    i = pl.program_id(0)
    j = pl.program_id(1)
    nj = pl.num_programs(1)
    seq = i // tiles_per_seq
    cr = chunk_rows
    r = lax.broadcasted_iota(jnp.int32, (cr, cr), 0)
    c = lax.broadcasted_iota(jnp.int32, (cr, cr), 1)
    ws = jnp.where(c <= r, ws_ref[:cr, :cr], 0.0).astype(BF16)
    bs = bs_ref[:cr, :]
    w = w_ref[...]

    def step(first, last):
        for r0 in range(0, u_ref.shape[0], row_chunk):
            rows = slice(r0, r0 + row_chunk)
            gated = []
            for c0 in range(r0, r0 + row_chunk, cr):
                sl = slice(c0, c0 + cr)
                vn = (v_ref[sl, :] - mean_ref[sl, :]) * rstd_ref[sl, :] * lg_ref[...] + lb_ref[...]
                if vn_ref is not None:
                    vn_ref[sl, :] = vn
                s = _dot(ws, vn.astype(BF16)) + bs
                gated.append(u_ref[sl, :].astype(F32) * s)
            tot = _dot(jnp.concatenate(gated, axis=0).astype(BF16), w)
            if not first:
                tot = o_ref[rows, :] + tot
            if last:
                tot = x_ref[rows, :] + _mod_rows(mod_ref, 2, seq, per_row, rows) * tot
            o_ref[rows, :] = tot

    @pl.when(j == 0)
    def _():
        step(True, False)

    @pl.when((j > 0) & (j < nj - 1))
    def _():
        step(False, False)

    @pl.when(j == nj - 1)
    def _():
        step(False, True)


def _sgu_out_call(u, v, mean, rstd, ln_g, ln_b, w_s, b_s, w_out, x, mod, *, layer, g, tm, per_row,
                  tiles_per_seq, chunk_rows, want_vn):
    m, half = u.shape
    d = x.shape[1]
    gd = half // SGU_GROUPS
    rows = mod.shape[2]
    bs3 = b_s.reshape(b_s.shape[0], SGU_GROUPS, CHUNK, 1)
    lg3 = ln_g.reshape(ln_g.shape[0], 1, half)
    lb3 = ln_b.reshape(ln_b.shape[0], 1, half)
    params = dict(per_row=per_row, tiles_per_seq=tiles_per_seq, chunk_rows=chunk_rows,
                  row_chunk=min(tm, ROW_CHUNK))

    def kern(*refs):
        if want_vn:
            _sgu_out_kernel(*refs, **params)
        else:
            _sgu_out_kernel(*refs, None, **params)

    blk = pl.BlockSpec((tm, gd), lambda i, j: (i, j))
    out_specs = [pl.BlockSpec((tm, d), lambda i, j: (i, 0))]
    out_shape = [jax.ShapeDtypeStruct((m, d), F32)]
    if want_vn:
        out_specs.append(blk)
        out_shape.append(jax.ShapeDtypeStruct((m, half), F32))
    return pl.pallas_call(
        kern,
        grid=(m // tm, SGU_GROUPS),
        in_specs=[
            blk,
            blk,
            pl.BlockSpec((tm, 1), lambda i, j: (i, 0)),
            pl.BlockSpec((tm, 1), lambda i, j: (i, 0)),
            pl.BlockSpec((None, 1, gd), lambda i, j: (g, 0, j)),
            pl.BlockSpec((None, 1, gd), lambda i, j: (g, 0, j)),
            pl.BlockSpec((None, None, CHUNK, CHUNK), lambda i, j: (g, j, 0, 0)),
            pl.BlockSpec((None, None, CHUNK, 1), lambda i, j: (g, j, 0, 0)),
            pl.BlockSpec((None, gd, d), lambda i, j: (g, j, 0)),
            pl.BlockSpec((tm, d), lambda i, j: (i, 0)),
            pl.BlockSpec((None, 3, rows, d), lambda i, j: (layer, 1, 0, 0)),
        ],
        out_specs=out_specs,
        out_shape=out_shape,
        compiler_params=_cparams(("arbitrary", "arbitrary")),
        name="sgu_gate_out",
    )(u, v, mean, rstd, lg3, lb3, w_s, bs3, w_out, x, mod)


def _rotary_table(pos):
    half = HEAD_DIM // 2
    inv_freq = ROPE_THETA ** (-jnp.arange(half, dtype=F32) / half)
    ang = pos.astype(F32)[:, None] * inv_freq[None, :]
    cos, sin = jnp.cos(ang), jnp.sin(ang)
    return jnp.concatenate([cos, cos, -sin, sin], axis=-1)


def kernel(x_prompt, x_sample, c_prompt, c_sample, cache_k, cache_v, page_table, norm_g, w_ada, b_ada,
           ffn_w_in, ffn_w_out, attn_w_qkv, attn_w_o, attn_lambda, attn_subln_g, sgu_w_in, sgu_ln_g,
           sgu_ln_b, sgu_w_s, sgu_b_s, sgu_w_out, final_g):
    nb, t, d = x_prompt.shape
    db, ts, _ = x_sample.shape
    depth = w_ada.shape[0]
    n_pages = page_table.shape[1]
    past = n_pages * PAGE_SIZE
    mp, ms = nb * t, db * ts
    tm = min(ROW_TILE, t)
    tps = t // tm

    attn_w_qkv, attn_w_o, sgu_w_in, sgu_w_out = (
        w.astype(BF16) for w in (attn_w_qkv, attn_w_o, sgu_w_in, sgu_w_out))
    ffn_w = {(0, 0): (ffn_w_in[0, 0].astype(BF16), ffn_w_out[0, 0].astype(BF16))}

    def cast_plan(sets):
        sets = [s for s in sets if s[0] < depth]
        return sets, [(w, li, wi) for li, wi in sets for w in (ffn_w_in, ffn_w_out)]

    def keep_ffn_weights(sets, casts):
        for k, key in enumerate(sets):
            ffn_w[key] = (casts[2 * k], casts[2 * k + 1])

    c_all = jnp.concatenate([c_prompt, c_sample, jnp.zeros((N_SEQ_PAD - nb - db, d), F32)], axis=0)
    mod = _ada_call(c_all, w_ada, b_ada)
    mod_s = jnp.repeat(mod[:, :, nb:nb + db], ts, axis=2)

    cs_p = _rotary_table(jnp.arange(t, dtype=jnp.int32))
    cs_s = jnp.tile(_rotary_table(past + jnp.arange(ts, dtype=jnp.int32)), (db, 1))

    xp = x_prompt.reshape(mp, d)
    xs = x_sample.reshape(ms, d)
    prm = dict(tm=tm, per_row=False, tiles_per_seq=tps)
    srm = dict(tm=ms, per_row=True, tiles_per_seq=1)
    kv_p = kv_s = None
    new_sgu_v = []

    for i in range(depth):
        last = i == depth - 1
        ffn = functools.partial(_ffn_call, norm_g=norm_g, final_g=final_g, layer=i)
        w_in0, w_out0 = ffn_w[(i, 0)]
        xp = ffn(xp, mod, w_in=w_in0, w_out=w_out0, sub=0, final_norm=False, **prm)
        xs = ffn(xs, mod_s, w_in=w_in0, w_out=w_out0, sub=0, final_norm=False, **srm)
        if i % 2 == 0:
            a = i // 2
            lam_init = 0.8 - 0.6 * math.exp(-0.3 * i)
            qp, kp, vp, kpb, vpt = _qkv_call(xp, mod, norm_g, attn_w_qkv, cs_p, kv_p, [],
                                             layer=i, a=a, prompt=True, **prm)
            kv_p = (kp, vp)
            sets, weights = cast_plan([(i, 1), (i + 1, 0), (i + 1, 1), (i + 2, 0)])
            op, *casts = _attn_call(qp, kpb, vpt, attn_lambda[a], attn_subln_g[a], weights,
                                    n_seq=nb, seq_len=t, lam_init=lam_init)
            keep_ffn_weights(sets, casts)
            xp = _proj_call(op, attn_w_o, xp, mod, layer=i, w_index=a, tn=512, **prm)
            qs, ks_, vs_ = _qkv_call(xs, mod_s, norm_g, attn_w_qkv, cs_s, kv_s, [], layer=i, a=a,
                                     prompt=False, **srm)
            kv_s = (ks_, vs_)
            os_ = _decode_call(page_table, qs, ks_[a], vs_[a], attn_lambda[a], attn_subln_g[a],
                               cache_k, cache_v, a=a, lam_init=lam_init)
            xs = _proj_call(os_, attn_w_o, xs, mod_s, layer=i, w_index=a, tn=512, **srm)
        else:
            g = i // 2
            up, vp_, mean_p, rstd_p = _sgu_in_call(xp, mod, norm_g, sgu_w_in, [], layer=i, g=g, **prm)
            (xp,) = _sgu_out_call(up, vp_, mean_p, rstd_p, sgu_ln_g, sgu_ln_b, sgu_w_s, sgu_b_s,
                                  sgu_w_out, xp, mod, layer=i, g=g, chunk_rows=min(CHUNK, t),
                                  want_vn=False, **prm)
            us, vs2, mean_s, rstd_s = _sgu_in_call(xs, mod_s, norm_g, sgu_w_in, [], layer=i, g=g, **srm)
            xs, vn_s = _sgu_out_call(us, vs2, mean_s, rstd_s, sgu_ln_g, sgu_ln_b, sgu_w_s, sgu_b_s,
                                     sgu_w_out, xs, mod_s, layer=i, g=g, chunk_rows=ts,
                                     want_vn=True, **srm)
            new_sgu_v.append(vn_s)
        w_in1, w_out1 = ffn_w[(i, 1)]
        xp = ffn(xp, mod, w_in=w_in1, w_out=w_out1, sub=2, final_norm=last, **prm)
        xs = ffn(xs, mod_s, w_in=w_in1, w_out=w_out1, sub=2, final_norm=last, **srm)

    n_sub = 2 * (d // V_DIM)
    y_prompt = xp.reshape(nb, t, d)
    y_sample = xs.reshape(db, ts, d)
    k_prompt = kv_p[0].reshape(-1, nb, t, n_sub, HEAD_DIM)
    v_prompt = kv_p[1].reshape(-1, nb, t, n_sub // 2, V_DIM)
    k_sample = kv_s[0].reshape(-1, db, ts, n_sub, HEAD_DIM)
    v_sample = kv_s[1].reshape(-1, db, ts, n_sub // 2, V_DIM)
    sgu_v_sample = jnp.stack(new_sgu_v).reshape(len(new_sgu_v), db, ts, -1)
    return (y_prompt, y_sample, k_prompt, v_prompt, k_sample, v_sample, sgu_v_sample)
```
